```python
import math
import jax, jax.numpy as jnp
from jax import lax
import numpy as np

D_MODEL = 1024
BATCH = 2
SEQ = 8192
DEPTH = 4
DEC_BATCH = 128
DEC_SEQ = 4
PAST_LEN = 8192
PAGE_SIZE = 128

N_EVEN = (DEPTH + 1) // 2
N_ODD = DEPTH // 2
MLA_HEADS = 8
MLA_NOPE = 64
MLA_ROPE = 32
MLA_V = 64
MLA_Q_LORA = 384
MLA_KV_LORA = 256
MLA_SCALE = (MLA_NOPE + MLA_ROPE) ** -0.5
ROPE_THETA = 10000.0
SB_HEADS = 8
SB_DIM = 64
SB_WIDTH = SB_HEADS * SB_DIM
HG_HEADS = 8
HG_DK = 64
HG_DV = 64
HG_CHUNK = 16
FORGET_FLOOR = 1e-30
SW_HEADS = 8
SW_KV_HEADS = 2
SW_GROUP = SW_HEADS // SW_KV_HEADS
SW_DIM = 64
WINDOW = 128
REL_BUCKETS = 32
REL_MAX_DIST = 128
D_FF = 4 * D_MODEL
Q_BLOCK = 128
NORM_EPS = 1e-6
MASK_VALUE = -1e30
EVEN_IN = MLA_Q_LORA + MLA_KV_LORA + MLA_ROPE + 3 * SB_WIDTH
EVEN_MIX = MLA_HEADS * MLA_V + SB_WIDTH
ODD_IN = 2 * HG_HEADS * HG_DK + 2 * HG_HEADS * HG_DV + (SW_HEADS + 2 * SW_KV_HEADS) * SW_DIM
ODD_MIX = HG_HEADS * HG_DV + SW_HEADS * SW_DIM

kernel_name = 'hybrid_mla_stickbreak_hgrn2_swa_step'


def rmsnorm(x, gain):
    xf = x.astype(jnp.float32)
    y = xf * lax.rsqrt(jnp.mean(xf * xf, axis=-1, keepdims=True) + NORM_EPS)
    return (y * gain.astype(jnp.float32)).astype(x.dtype)


def split_cols(h, sizes):
    return jnp.split(h, [int(c) for c in np.cumsum(sizes)[:-1]], axis=-1)


def rope(x, pos):
    half = x.shape[-1] // 2
    inv = ROPE_THETA ** (-jnp.arange(half, dtype=jnp.float32) / half)
    ang = pos.astype(jnp.float32)[:, None] * inv[None, :]
    cos = jnp.cos(ang)[None, :, None, :]
    sin = jnp.sin(ang)[None, :, None, :]
    xf = x.astype(jnp.float32)
    x1, x2 = xf[..., :half], xf[..., half:]
    return jnp.concatenate([x1 * cos - x2 * sin, x2 * cos + x1 * sin], axis=-1).astype(x.dtype)


def rel_bucket(rel):
    n = jnp.maximum(rel, 0)
    exact = REL_BUCKETS // 2
    scaled = jnp.log(jnp.maximum(n, 1).astype(jnp.float32) / exact) / math.log(REL_MAX_DIST / exact)
    large = jnp.minimum(exact + (jnp.maximum(scaled, 0.0) * (REL_BUCKETS - exact)).astype(jnp.int32), REL_BUCKETS - 1)
    return jnp.where(n < exact, n, large)


def gather_pages(pool, layer, page_table):
    g = pool[layer][page_table]
    return g.reshape((g.shape[0], g.shape[1] * g.shape[2]) + g.shape[3:])


def sweep_query_blocks(fn, qs, qpos):
    B, S = qs[0].shape[:2]
    nb = S // Q_BLOCK
    blk = tuple(a.reshape((B, nb, Q_BLOCK) + a.shape[2:]).swapaxes(0, 1) for a in qs)
    out = lax.map(lambda args: fn(*args), blk + (qpos.reshape(nb, Q_BLOCK),))
    return out.swapaxes(0, 1).reshape((B, S) + out.shape[3:])


def mla_prompt(q_nope, q_rope, ckv, krope, w_kvb):
    B, S = ckv.shape[:2]
    wkv = w_kvb.reshape(MLA_KV_LORA, MLA_HEADS, MLA_NOPE + MLA_V)
    k_nope = jnp.einsum('bsc,chn->bshn', ckv, wkv[..., :MLA_NOPE])
    v = jnp.einsum('bsc,chv->bshv', ckv, wkv[..., MLA_NOPE:])
    kpos = jnp.arange(S)

    def block(qn, qr, qpos):
        logits = (jnp.einsum('bthn,bshn->bhts', qn, k_nope)
                  + jnp.einsum('bthr,bsr->bhts', qr, krope)).astype(jnp.float32) * MLA_SCALE
        logits = jnp.where(kpos[None, :] <= qpos[:, None], logits, MASK_VALUE)
        p = jax.nn.softmax(logits, axis=-1).astype(v.dtype)
        return jnp.einsum('bhts,bshv->bthv', p, v)

    out = sweep_query_blocks(block, (q_nope, q_rope), kpos)
    return out.reshape(B, S, MLA_HEADS * MLA_V)


def mla_sample(q_nope, q_rope, ckv, krope, ckv_past, krope_past, w_kvb):
    B, T = ckv.shape[:2]
    P = ckv_past.shape[1]
    wkv = w_kvb.reshape(MLA_KV_LORA, MLA_HEADS, MLA_NOPE + MLA_V)
    q_lat = jnp.einsum('bthn,chn->bthc', q_nope, wkv[..., :MLA_NOPE])

    def scores(c, r):
        return jnp.einsum('bthc,bsc->bhts', q_lat, c) + jnp.einsum('bthr,bsr->bhts', q_rope, r)

    logits = jnp.concatenate([scores(ckv_past, krope_past), scores(ckv, krope)], axis=-1).astype(jnp.float32) * MLA_SCALE
    kpos = jnp.arange(P + T)
    qpos = P + jnp.arange(T)
    logits = jnp.where(kpos[None, :] <= qpos[:, None], logits, MASK_VALUE)
    p = jax.nn.softmax(logits, axis=-1).astype(ckv.dtype)
    o_lat = jnp.einsum('bhts,bsc->bthc', p[..., :P], ckv_past) + jnp.einsum('bhts,bsc->bthc', p[..., P:], ckv)
    return jnp.einsum('bthc,chv->bthv', o_lat, wkv[..., MLA_NOPE:]).reshape(B, T, MLA_HEADS * MLA_V)


def sb_weights(z, allowed):
    log_beta = jax.nn.log_sigmoid(z)
    log_1mb = jnp.where(allowed, jax.nn.log_sigmoid(-z), 0.0)
    suffix = lax.cumsum(log_1mb, axis=z.ndim - 1, reverse=True) - log_1mb
    return jnp.where(allowed, jnp.exp(jnp.where(allowed, log_beta + suffix, 0.0)), 0.0)


def sb_prompt(q, k, v):
    B, S = q.shape[:2]
    kpos = jnp.arange(S)

    def block(qb, qpos):
        z = jnp.einsum('bthd,bshd->bhts', qb, k).astype(jnp.float32) * SB_DIM ** -0.5
        a = sb_weights(z, kpos[None, :] < qpos[:, None]).astype(v.dtype)
        return jnp.einsum('bhts,bshd->bthd', a, v)

    return sweep_query_blocks(block, (q,), kpos).reshape(B, S, SB_WIDTH)


def sb_sample(q, k, v, k_past, v_past):
    B, T = q.shape[:2]
    P = k_past.shape[1]
    z = jnp.concatenate([jnp.einsum('bthd,bshd->bhts', q, k_past),
                         jnp.einsum('bthd,bshd->bhts', q, k)], axis=-1).astype(jnp.float32) * SB_DIM ** -0.5
    kpos = jnp.arange(P + T)
    qpos = P + jnp.arange(T)
    a = sb_weights(z, kpos[None, :] < qpos[:, None]).astype(v.dtype)
    out = jnp.einsum('bhts,bshd->bthd', a[..., :P], v_past) + jnp.einsum('bhts,bshd->bthd', a[..., P:], v)
    return out.reshape(B, T, SB_WIDTH)


def hgrn2_chunked(q, k, v, g, s0):
    B, T, H = q.shape[:3]
    L = min(HG_CHUNK, T)
    pad = (-T) % L
    padt = lambda a: jnp.pad(a, ((0, 0), (0, pad), (0, 0), (0, 0)))
    q, k, v, g = padt(q), padt(k), padt(v), padt(g)
    nc = (T + pad) // L
    to_chunks = lambda a: a.reshape(B, nc, L, H, a.shape[-1]).transpose(1, 0, 3, 2, 4)
    tri = jnp.tril(jnp.ones((L, L), dtype=bool))

    def step(S, xs):
        qc, kc, vc, gc = xs
        G = jnp.cumsum(gc, axis=2)
        o_inter = jnp.einsum('bhld,bhde->bhle', qc * jnp.exp(G), S)
        diff = G[:, :, :, None, :] - G[:, :, None, :, :]
        decay = jnp.where(tri[:, :, None], jnp.exp(jnp.where(tri[:, :, None], diff, 0.0)), 0.0)
        A = jnp.einsum('bhtd,bhsd,bhtsd->bhts', qc, kc, decay)
        o = o_inter + jnp.einsum('bhts,bhse->bhte', A, vc)
        GL = G[:, :, -1:, :]
        S = jnp.exp(GL[:, :, 0])[..., None] * S + jnp.einsum('bhsd,bhse->bhde', kc * jnp.exp(GL - G), vc)
        return S, o

    S, o = lax.scan(step, s0.astype(jnp.float32), (to_chunks(q), to_chunks(k), to_chunks(v), to_chunks(g)))
    o = o.transpose(1, 0, 3, 2, 4).reshape(B, nc * L, H, v.shape[-1])[:, :T]
    return o, S


def hgrn_readout(o, gate, gnorm):
    B, T = o.shape[:2]
    y = rmsnorm(o, gnorm) * jax.nn.silu(gate.astype(jnp.float32))
    return y.reshape(B, T, HG_HEADS * HG_DV)


def swa_attend(q, k, v, rel, allowed, rel_bias, sinks):
    N, Tq = q.shape[:2]
    Tk = k.shape[1]
    qg = q.reshape(N, Tq, SW_KV_HEADS, SW_GROUP, SW_DIM)
    logits = jnp.einsum('nqkgd,nskd->nkgqs', qg, k).astype(jnp.float32) * SW_DIM ** -0.5
    bias = rel_bias[rel_bucket(rel)].astype(jnp.float32).transpose(2, 0, 1).reshape(SW_KV_HEADS, SW_GROUP, Tq, Tk)
    logits = jnp.where(allowed[:, None, None], logits + bias, MASK_VALUE)
    sink = jnp.broadcast_to(sinks.astype(jnp.float32).reshape(1, SW_KV_HEADS, SW_GROUP, 1, 1), logits.shape[:-1] + (1,))
    p = jax.nn.softmax(jnp.concatenate([logits, sink], axis=-1), axis=-1)[..., :-1].astype(v.dtype)
    out = jnp.einsum('nkgqs,nskd->nqkgd', p, v)
    return out.reshape(N, Tq, SW_HEADS * SW_DIM)


def swa_prompt(q, k, v, rel_bias, sinks):
    B, S = q.shape[:2]
    W = WINDOW
    nb = S // W
    padf = lambda a: jnp.pad(a, ((0, 0), (W, 0), (0, 0), (0, 0)))[:, :S]
    band = lambda a: jnp.concatenate([padf(a).reshape(B, nb, W, SW_KV_HEADS, SW_DIM),
                                      a.reshape(B, nb, W, SW_KV_HEADS, SW_DIM)], axis=2).reshape(B * nb, 2 * W, SW_KV_HEADS, SW_DIM)
    qi = jnp.arange(W)
    kj = jnp.arange(2 * W)
    rel = (qi[:, None] + W) - kj[None, :]
    kpos = jnp.arange(nb)[:, None] * W - W + kj[None, :]
    allowed = (rel >= 0)[None] & (rel <= WINDOW)[None] & (kpos[:, None, :] >= 0)
    allowed = jnp.broadcast_to(allowed[None], (B, nb, W, 2 * W)).reshape(B * nb, W, 2 * W)
    out = swa_attend(q.reshape(B * nb, W, SW_HEADS, SW_DIM), band(k), band(v), rel, allowed, rel_bias, sinks)
    return out.reshape(B, S, SW_HEADS * SW_DIM)


def swa_sample(q, k, v, k_buf, v_buf, rel_bias, sinks):
    T = q.shape[1]
    nbuf = k_buf.shape[1]
    kk = jnp.concatenate([k_buf, k], axis=1)
    vv = jnp.concatenate([v_buf, v], axis=1)
    kpos = PAST_LEN - nbuf + jnp.arange(nbuf + T)
    qpos = PAST_LEN + jnp.arange(T)
    rel = qpos[:, None] - kpos[None, :]
    allowed = ((rel >= 0) & (rel <= WINDOW))[None]
    out = swa_attend(q, kk, vv, rel, allowed, rel_bias, sinks)
    return out, kk[:, -nbuf:], vv[:, -nbuf:]


def even_project(h, pos, q_norm, w_qb, kv_norm):
    B, T = h.shape[:2]
    q_a, kv_a, k_r, sq, sk, sv = split_cols(h, [MLA_Q_LORA, MLA_KV_LORA, MLA_ROPE, SB_WIDTH, SB_WIDTH, SB_WIDTH])
    q = (rmsnorm(q_a, q_norm) @ w_qb).reshape(B, T, MLA_HEADS, MLA_NOPE + MLA_ROPE)
    q_nope = q[..., :MLA_NOPE]
    q_rope = rope(q[..., MLA_NOPE:], pos)
    ckv = rmsnorm(kv_a, kv_norm)
    krope = rope(k_r[:, :, None, :], pos)[:, :, 0, :]
    heads = lambda a: a.reshape(B, T, SB_HEADS, SB_DIM)
    return q_nope, q_rope, ckv, krope, heads(sq), heads(sk), heads(sv)


def odd_project(h, lb):
    B, T = h.shape[:2]
    hq, hf, hi, hg, dq, dk, dv = split_cols(h, [HG_HEADS * HG_DK, HG_HEADS * HG_DK, HG_HEADS * HG_DV, HG_HEADS * HG_DV,
                                                SW_HEADS * SW_DIM, SW_KV_HEADS * SW_DIM, SW_KV_HEADS * SW_DIM])
    f = lb + (1.0 - lb) * jax.nn.sigmoid(hf.astype(jnp.float32))
    log_f = jnp.log(jnp.maximum(f, FORGET_FLOOR))
    k = -jnp.expm1(log_f)
    q = jax.nn.silu(hq.astype(jnp.float32))
    hh = lambda a, d: a.reshape(B, T, HG_HEADS, d)
    sh = lambda a, n: a.reshape(B, T, n, SW_DIM)
    hg_parts = (hh(q, HG_DK), hh(k, HG_DK), hh(hi.astype(jnp.float32), HG_DV), hh(log_f, HG_DK), hh(hg, HG_DV))
    sw_parts = (sh(dq, SW_HEADS), sh(dk, SW_KV_HEADS), sh(dv, SW_KV_HEADS))
    return hg_parts, sw_parts


def sq_relu_mlp(h, w_up, w_down):
    return jnp.square(jax.nn.relu(h @ w_up)) @ w_down


def setup_inputs(seed: int = 0) -> dict:
    key = jax.random.key(seed)
    ks = jax.random.split(key, 27)
    nrm = lambda k, shape, s: s * jax.random.normal(k, shape, jnp.float32)
    gain = lambda k, shape: 1.0 + 0.02 * jax.random.normal(k, shape, jnp.float32)
    n_pages = PAST_LEN // PAGE_SIZE
    n_used = DEC_BATCH * n_pages
    n_pool = n_used + max(n_used // 4, 1)
    page_table = jax.random.permutation(ks[9], n_pool)[:n_used].reshape(DEC_BATCH, n_pages).astype(jnp.int32)
    w_buf = min(WINDOW, PAST_LEN)
    return {
        'x_prompt': nrm(ks[0], (BATCH, SEQ, D_MODEL), 1.0),
        'x_sample': nrm(ks[1], (DEC_BATCH, DEC_SEQ, D_MODEL), 1.0),
        'cache_mla_ckv': nrm(ks[2], (N_EVEN, n_pool, PAGE_SIZE, MLA_KV_LORA), 1.0),
        'cache_mla_krope': nrm(ks[3], (N_EVEN, n_pool, PAGE_SIZE, MLA_ROPE), 1.0),
        'cache_sb_k': nrm(ks[4], (N_EVEN, n_pool, PAGE_SIZE, SB_HEADS, SB_DIM), 1.0),
        'cache_sb_v': nrm(ks[5], (N_EVEN, n_pool, PAGE_SIZE, SB_HEADS, SB_DIM), 1.0),
        'state_hgrn': nrm(ks[6], (N_ODD, DEC_BATCH, HG_HEADS, HG_DK, HG_DV), 0.5),
        'state_swa_k': nrm(ks[7], (N_ODD, DEC_BATCH, w_buf, SW_KV_HEADS, SW_DIM), 1.0),
        'state_swa_v': nrm(ks[8], (N_ODD, DEC_BATCH, w_buf, SW_KV_HEADS, SW_DIM), 1.0),
        'page_table': page_table,
        'mix_norm': gain(ks[10], (DEPTH, D_MODEL)),
        'w_in_even': nrm(ks[11], (N_EVEN, D_MODEL, EVEN_IN), D_MODEL ** -0.5),
        'mla_q_norm': gain(ks[12], (N_EVEN, MLA_Q_LORA)),
        'mla_w_qb': nrm(ks[13], (N_EVEN, MLA_Q_LORA, MLA_HEADS * (MLA_NOPE + MLA_ROPE)), MLA_Q_LORA ** -0.5),
        'mla_kv_norm': gain(ks[14], (N_EVEN, MLA_KV_LORA)),
        'mla_w_kvb': nrm(ks[15], (N_EVEN, MLA_KV_LORA, MLA_HEADS * (MLA_NOPE + MLA_V)), MLA_KV_LORA ** -0.5),
        'w_out_even': nrm(ks[16], (N_EVEN, EVEN_MIX, D_MODEL), EVEN_MIX ** -0.5),
        'w_in_odd': nrm(ks[17], (N_ODD, D_MODEL, ODD_IN), D_MODEL ** -0.5),
        'hgrn_lb_logits': nrm(ks[18], (N_ODD, HG_HEADS * HG_DK), 1.0),
        'hgrn_out_norm': gain(ks[19], (N_ODD, HG_DV)),
        'swa_sinks': nrm(ks[20], (N_ODD, SW_HEADS), 0.5),
        'w_out_odd': nrm(ks[21], (N_ODD, ODD_MIX, D_MODEL), ODD_MIX ** -0.5),
        'rel_bias': nrm(ks[22], (REL_BUCKETS, SW_HEADS), 0.5),
        'mlp_norm': gain(ks[23], (DEPTH, D_MODEL)),
        'w_up': nrm(ks[24], (DEPTH, D_MODEL, D_FF), D_MODEL ** -0.5),
        'w_down': nrm(ks[25], (DEPTH, D_FF, D_MODEL), D_FF ** -0.5),
        'final_norm': gain(ks[26], (D_MODEL,)),
    }


def reference(x_prompt, x_sample, cache_mla_ckv, cache_mla_krope, cache_sb_k, cache_sb_v, state_hgrn,
              state_swa_k, state_swa_v, page_table, mix_norm, w_in_even, mla_q_norm, mla_w_qb, mla_kv_norm,
              mla_w_kvb, w_out_even, w_in_odd, hgrn_lb_logits, hgrn_out_norm, swa_sinks, w_out_odd, rel_bias,
              mlp_norm, w_up, w_down, final_norm):
    lb_p = jax.nn.softmax(hgrn_lb_logits.astype(jnp.float32), axis=0)
    lower_bounds = jnp.cumsum(lb_p, axis=0) - lb_p[0]
    pos_p = jnp.arange(x_prompt.shape[1])
    pos_s = PAST_LEN + jnp.arange(x_sample.shape[1])
    xp, xs = x_prompt, x_sample
    ckv_p, kr_p, sbk_p, sbv_p, hg_p, swk_p, swv_p = [], [], [], [], [], [], []
    ckv_s, kr_s, sbk_s, sbv_s, hg_s, swk_s, swv_s = [], [], [], [], [], [], []
    for layer in range(DEPTH):
        if layer % 2 == 0:
            e = layer // 2
            hp = rmsnorm(xp, mix_norm[layer]) @ w_in_even[e]
            hs = rmsnorm(xs, mix_norm[layer]) @ w_in_even[e]
            qn_p, qr_p, c_p, r_p, bq_p, bk_p, bv_p = even_project(hp, pos_p, mla_q_norm[e], mla_w_qb[e], mla_kv_norm[e])
            qn_s, qr_s, c_s, r_s, bq_s, bk_s, bv_s = even_project(hs, pos_s, mla_q_norm[e], mla_w_qb[e], mla_kv_norm[e])
            a_p = mla_prompt(qn_p, qr_p, c_p, r_p, mla_w_kvb[e])
            b_p = sb_prompt(bq_p, bk_p, bv_p)
            a_s = mla_sample(qn_s, qr_s, c_s, r_s, gather_pages(cache_mla_ckv, e, page_table),
                             gather_pages(cache_mla_krope, e, page_table), mla_w_kvb[e])
            b_s = sb_sample(bq_s, bk_s, bv_s, gather_pages(cache_sb_k, e, page_table),
                            gather_pages(cache_sb_v, e, page_table))
            yp = jnp.concatenate([a_p, b_p], axis=-1) @ w_out_even[e]
            ys = jnp.concatenate([a_s, b_s], axis=-1) @ w_out_even[e]
            ckv_p.append(c_p); kr_p.append(r_p); sbk_p.append(bk_p); sbv_p.append(bv_p)
            ckv_s.append(c_s); kr_s.append(r_s); sbk_s.append(bk_s); sbv_s.append(bv_s)
        else:
            o = layer // 2
            lb = lower_bounds[o]
            hp = rmsnorm(xp, mix_norm[layer]) @ w_in_odd[o]
            hs = rmsnorm(xs, mix_norm[layer]) @ w_in_odd[o]
            (cq_p, ck_p, cv_p, cg_p, cz_p), (dq_p, dk_p, dv_p) = odd_project(hp, lb)
            (cq_s, ck_s, cv_s, cg_s, cz_s), (dq_s, dk_s, dv_s) = odd_project(hs, lb)
            s0_p = jnp.zeros((xp.shape[0], HG_HEADS, HG_DK, HG_DV), jnp.float32)
            co_p, S_p = hgrn2_chunked(cq_p, ck_p, cv_p, cg_p, s0_p)
            co_s, S_s = hgrn2_chunked(cq_s, ck_s, cv_s, cg_s, state_hgrn[o])
            c_out_p = hgrn_readout(co_p, cz_p, hgrn_out_norm[o]).astype(xp.dtype)
            c_out_s = hgrn_readout(co_s, cz_s, hgrn_out_norm[o]).astype(xs.dtype)
            d_out_p = swa_prompt(dq_p, dk_p, dv_p, rel_bias, swa_sinks[o])
            d_out_s, kb_s, vb_s = swa_sample(dq_s, dk_s, dv_s, state_swa_k[o], state_swa_v[o], rel_bias, swa_sinks[o])
            yp = jnp.concatenate([c_out_p, d_out_p], axis=-1) @ w_out_odd[o]
            ys = jnp.concatenate([c_out_s, d_out_s], axis=-1) @ w_out_odd[o]
            nkeep = min(WINDOW, xp.shape[1])
            hg_p.append(S_p.astype(xp.dtype)); swk_p.append(dk_p[:, -nkeep:]); swv_p.append(dv_p[:, -nkeep:])
            hg_s.append(S_s.astype(xs.dtype)); swk_s.append(kb_s); swv_s.append(vb_s)
        xp = xp + yp
        xs = xs + ys
        xp = xp + sq_relu_mlp(rmsnorm(xp, mlp_norm[layer]), w_up[layer], w_down[layer])
        xs = xs + sq_relu_mlp(rmsnorm(xs, mlp_norm[layer]), w_up[layer], w_down[layer])
    y_prompt = rmsnorm(xp, final_norm)
    y_sample = rmsnorm(xs, final_norm)
    st = lambda lst: jnp.stack(lst, axis=0)
    return (y_prompt, y_sample,
            st(ckv_p), st(kr_p), st(sbk_p), st(sbv_p), st(hg_p), st(swk_p), st(swv_p),
            st(ckv_s), st(kr_s), st(sbk_s), st(sbv_s), st(hg_s), st(swk_s), st(swv_s))
```

```python
import functools
import math

import numpy as np
import jax
import jax.numpy as jnp
from jax import lax
from jax.experimental import pallas as pl
from jax.experimental.pallas import tpu as pltpu

F32 = jnp.float32
BF16 = jnp.bfloat16

D_MODEL = 1024
PAGE_SIZE = 128
MLA_HEADS = 8
MLA_NOPE = 64
MLA_ROPE = 32
MLA_V = 64
MLA_Q_LORA = 384
MLA_KV_LORA = 256
MLA_SCALE = (MLA_NOPE + MLA_ROPE) ** -0.5
ROPE_THETA = 10000.0
SB_HEADS = 8
SB_DIM = 64
SB_WIDTH = SB_HEADS * SB_DIM
SB_SCALE = SB_DIM ** -0.5
HG_HEADS = 8
HG_DK = 64
HG_DV = 64
HG_SUB = 16
FORGET_FLOOR = 1e-30
SW_HEADS = 8
SW_KV_HEADS = 2
SW_GROUP = SW_HEADS // SW_KV_HEADS
SW_DIM = 64
SW_SCALE = SW_DIM ** -0.5
WINDOW = 128
REL_BUCKETS = 32
REL_MAX_DIST = 128
D_FF = 4 * D_MODEL
NORM_EPS = 1e-6
MASK_VALUE = -1e30
SB_EXIT = -150.0

LANES = 128
HEAD_PAD = 128
VMEM_LIMIT = 56 * 1024 * 1024

NT = (((1,), (1,)), ((), ()))


def _dot(a, b):
    return jnp.dot(a, b, preferred_element_type=F32)


def _dot_nt(a, b):
    return lax.dot_general(a, b, NT, preferred_element_type=F32)


def _rms(x, g):
    return x * lax.rsqrt(jnp.mean(x * x, axis=-1, keepdims=True) + NORM_EPS) * g


def _split2(x):
    hi = x.astype(BF16)
    lo = (x - hi.astype(F32)).astype(BF16)
    return hi, lo


def _params(sem):
    return pltpu.CompilerParams(dimension_semantics=sem, vmem_limit_bytes=VMEM_LIMIT)


def _full(shape):
    n = len(shape)
    return pl.BlockSpec(shape, lambda *_: (0,) * n)


EV_QA = 0
EV_KVA = MLA_Q_LORA
EV_SQ = EV_KVA + MLA_KV_LORA
EV_SK = EV_SQ + SB_WIDTH
EV_SV = EV_SK + SB_WIDTH
EV_KR = EV_SV + SB_WIDTH
EV_KRS = EV_KR + LANES
EV_N = EV_KRS + LANES


def _even_common(x_ref, g_ref, win_ref, qn_ref, wq_ref, wqs_ref, kvn_ref, tab_ref):
    h = _rms(x_ref[...], g_ref[...]).astype(BF16)
    hh = _dot(h, win_ref[...])
    tab = tab_ref[...]
    cos_q, sin_q = tab[:, 0:LANES], tab[:, LANES:2 * LANES]
    cos_k, sin_k = tab[:, 2 * LANES:3 * LANES], tab[:, 3 * LANES:4 * LANES]
    qa = _rms(hh[:, EV_QA:EV_KVA], qn_ref[...]).astype(BF16)
    q = (_dot(qa, wq_ref[...]) * jnp.tile(cos_q, (1, MLA_HEADS))
         + _dot(qa, wqs_ref[...]) * jnp.tile(sin_q, (1, MLA_HEADS)))
    ckv = _rms(hh[:, EV_KVA:EV_SQ], kvn_ref[...])
    kr = hh[:, EV_KR:EV_KRS] * cos_k + hh[:, EV_KRS:EV_N] * sin_k
    sq = (hh[:, EV_SQ:EV_SK] * SB_SCALE).astype(BF16)
    return q, ckv, kr, sq, hh[:, EV_SK:EV_SV], hh[:, EV_SV:EV_KR]


def _even_prompt_kernel(x_ref, g_ref, win_ref, qn_ref, wq_ref, wqs_ref, kvn_ref, tab_ref,
                        wk_ref, wv_ref, e_ref,
                        q_out, k_out, v_out, ckv_out, kr_out, sq_out, sk_out, sv_out):
    q, ckv, kr, sq, sk, sv = _even_common(x_ref, g_ref, win_ref, qn_ref, wq_ref, wqs_ref, kvn_ref, tab_ref)
    q_out[...] = q.astype(BF16)
    ckv_out[...] = ckv
    kr_out[...] = kr[:, :MLA_ROPE]
    sq_out[...] = sq
    sk_out[...] = sk
    sv_out[...] = sv
    cb = ckv.astype(BF16)
    k_out[...] = (_dot(cb, wk_ref[...]) + _dot(kr.astype(BF16), e_ref[...])).astype(BF16)
    v_out[...] = _dot(cb, wv_ref[...]).astype(BF16)


def _even_sample_kernel(x_ref, g_ref, win_ref, qn_ref, wq_ref, wqs_ref, kvn_ref, tab_ref,
                        wabs_ref,
                        q_out, qlat_out, ckv_out, kr_out, sq_out, sk_out, sv_out):
    q, ckv, kr, sq, sk, sv = _even_common(x_ref, g_ref, win_ref, qn_ref, wq_ref, wqs_ref, kvn_ref, tab_ref)
    qb = q.astype(BF16)
    q_out[...] = qb
    qlat_out[...] = _dot(qb, wabs_ref[...]).astype(BF16)
    ckv_out[...] = ckv
    kr_out[...] = kr[:, :MLA_ROPE]
    sq_out[...] = sq
    sk_out[...] = sk
    sv_out[...] = sv


def _rope_tables(pos):
    half = MLA_ROPE // 2
    inv = ROPE_THETA ** (-jnp.arange(half, dtype=F32) / half)
    ang = pos.astype(F32)[:, None] * inv[None, :]
    cos = jnp.cos(ang)
    sin = jnp.sin(ang)
    cos2 = jnp.concatenate([cos, cos], axis=-1)
    sin2 = jnp.concatenate([sin, sin], axis=-1)
    n = pos.shape[0]
    one = jnp.ones((n, MLA_NOPE), F32)
    z = lambda w: jnp.zeros((n, w), F32)
    cos_q = jnp.concatenate([one, cos2, z(HEAD_PAD - MLA_NOPE - MLA_ROPE)], -1) * MLA_SCALE
    sin_q = jnp.concatenate([z(MLA_NOPE), sin2, z(HEAD_PAD - MLA_NOPE - MLA_ROPE)], -1) * MLA_SCALE
    cos_k = jnp.concatenate([cos2, z(LANES - MLA_ROPE)], -1)
    sin_k = jnp.concatenate([sin2, z(LANES - MLA_ROPE)], -1)
    return jnp.concatenate([cos_q, sin_q, cos_k, sin_k], -1)


def _rot_cols(w):
    half = w.shape[-1] // 2
    return jnp.concatenate([-w[..., half:], w[..., :half]], axis=-1)


def _even_weights(w_in, w_qb, w_kvb):
    d = w_in.shape[0]
    o_kva, o_kr = MLA_Q_LORA, MLA_Q_LORA + MLA_KV_LORA
    o_sq = o_kr + MLA_ROPE
    w_kr = w_in[:, o_kr:o_sq]
    zpad = jnp.zeros((d, LANES - MLA_ROPE), F32)
    win = jnp.concatenate([w_in[:, :o_kr], w_in[:, o_sq:], w_kr, zpad, _rot_cols(w_kr), zpad], -1).astype(BF16)
    wq3 = w_qb.reshape(MLA_Q_LORA, MLA_HEADS, MLA_NOPE + MLA_ROPE)
    nope, rp = wq3[..., :MLA_NOPE], wq3[..., MLA_NOPE:]
    zq = lambda w: jnp.zeros((MLA_Q_LORA, MLA_HEADS, w), F32)
    pad = HEAD_PAD - MLA_NOPE - MLA_ROPE
    wq = jnp.concatenate([nope, rp, zq(pad)], -1).reshape(MLA_Q_LORA, -1).astype(BF16)
    wqs = jnp.concatenate([zq(MLA_NOPE), _rot_cols(rp), zq(pad)], -1).reshape(MLA_Q_LORA, -1).astype(BF16)
    wkv3 = w_kvb.reshape(MLA_KV_LORA, MLA_HEADS, MLA_NOPE + MLA_V)
    knope, wv3 = wkv3[..., :MLA_NOPE], wkv3[..., MLA_NOPE:]
    wk = jnp.concatenate([knope, jnp.zeros((MLA_KV_LORA, MLA_HEADS, HEAD_PAD - MLA_NOPE), F32)], -1)
    wk = wk.reshape(MLA_KV_LORA, -1).astype(BF16)
    wv = wv3.reshape(MLA_KV_LORA, -1).astype(BF16)
    eye = jnp.eye(MLA_HEADS, dtype=F32)
    kt = jnp.concatenate([knope, jnp.zeros((MLA_KV_LORA, MLA_HEADS, HEAD_PAD - MLA_NOPE), F32)], -1)
    wabs = jnp.einsum('chn,hg->hngc', kt, eye).reshape(MLA_HEADS * HEAD_PAD, MLA_HEADS * MLA_KV_LORA).astype(BF16)
    wvbd = jnp.einsum('chv,hg->hcgv', wv3, eye).reshape(MLA_HEADS * MLA_KV_LORA, MLA_HEADS * MLA_V).astype(BF16)
    return win, wq, wqs, wk, wv, wabs, wvbd


def _krope_place():
    e = np.zeros((LANES, MLA_HEADS * HEAD_PAD), np.float32)
    for h in range(MLA_HEADS):
        for i in range(MLA_ROPE):
            e[i, h * HEAD_PAD + MLA_NOPE + i] = 1.0
    return jnp.asarray(e, BF16)


def _even_proj(x, gain, weights, q_norm, kv_norm, tab, sample, tm):
    m = x.shape[0]
    win, wq, wqs, wk, wv, wabs, _ = weights
    hq = MLA_HEADS * HEAD_PAD
    row = lambda w: pl.BlockSpec((tm, w), lambda i: (i, 0))
    common_in = [row(D_MODEL), _full((1, D_MODEL)), _full(win.shape), _full((1, MLA_Q_LORA)), _full(wq.shape),
                 _full(wqs.shape), _full((1, MLA_KV_LORA)), row(4 * LANES)]
    common_args = [x, gain.reshape(1, -1), win, q_norm.reshape(1, -1), wq, wqs, kv_norm.reshape(1, -1), tab]
    sds = jax.ShapeDtypeStruct
    tail_shapes = [sds((m, MLA_KV_LORA), F32), sds((m, MLA_ROPE), F32), sds((m, SB_WIDTH), BF16),
                   sds((m, SB_WIDTH), F32), sds((m, SB_WIDTH), F32)]
    tail_specs = [row(MLA_KV_LORA), row(MLA_ROPE), row(SB_WIDTH), row(SB_WIDTH), row(SB_WIDTH)]
    if sample:
        return pl.pallas_call(
            _even_sample_kernel, grid=(m // tm,),
            in_specs=common_in + [_full(wabs.shape)],
            out_specs=[row(hq), row(MLA_HEADS * MLA_KV_LORA)] + tail_specs,
            out_shape=[sds((m, hq), BF16), sds((m, MLA_HEADS * MLA_KV_LORA), BF16)] + tail_shapes,
            compiler_params=_params(("parallel",)), name="even_proj_sample",
        )(*common_args, wabs)
    e = _krope_place()
    return pl.pallas_call(
        _even_prompt_kernel, grid=(m // tm,),
        in_specs=common_in + [_full(wk.shape), _full(wv.shape), _full(e.shape)],
        out_specs=[row(hq), row(hq), row(MLA_HEADS * MLA_V)] + tail_specs,
        out_shape=[sds((m, hq), BF16), sds((m, hq), BF16), sds((m, MLA_HEADS * MLA_V), BF16)] + tail_shapes,
        compiler_params=_params(("parallel",)), name="even_proj_prompt",
    )(*common_args, wk, wv, e)


def _mla_prompt_kernel(q_ref, k_ref, v_ref, o_ref, *, t):
    qi = pl.program_id(2)
    q = q_ref[0]
    qh = (q[:, :HEAD_PAD], q[:, HEAD_PAD:])
    head1 = lax.broadcasted_iota(jnp.int32, (1, LANES), 1) >= MLA_V
    rows = lax.broadcasted_iota(jnp.int32, (t, t), 0)
    cols = lax.broadcasted_iota(jnp.int32, (t, t), 1)

    def block(jb, carry, diag):
        m0, l0, m1, l1, acc = carry
        ks = pl.multiple_of(jb * t, t)
        k = k_ref[0, pl.ds(ks, t), :]
        v = v_ref[0, pl.ds(ks, t), :]
        new = []
        pvs = []
        alphas = []
        for h, (m, l) in enumerate(((m0, l0), (m1, l1))):
            s = _dot_nt(qh[h], k[:, h * HEAD_PAD:(h + 1) * HEAD_PAD])
            if diag:
                s = jnp.where(cols <= rows, s, MASK_VALUE)
            mn = jnp.maximum(m, jnp.max(s, axis=-1, keepdims=True))
            a = jnp.exp(m - mn)
            p = jnp.exp(s - mn)
            new += [mn, a * l + jnp.sum(p, axis=-1, keepdims=True)]
            pvs.append(_dot(p.astype(BF16), v))
            alphas.append(a)
        acc = jnp.where(head1, alphas[1], alphas[0]) * acc + jnp.where(head1, pvs[1], pvs[0])
        return new[0], new[1], new[2], new[3], acc

    neg = jnp.full((t, 1), MASK_VALUE, F32)
    zero = jnp.zeros((t, 1), F32)
    init = (neg, zero, neg, zero, jnp.zeros((t, LANES), F32))
    carry = lax.fori_loop(0, qi, lambda jb, c: block(jb, c, False), init)
    _, l0, _, l1, acc = block(qi, carry, True)
    o_ref[0] = acc / jnp.where(head1, l1, l0)


def _mla_prompt(q, k, v, t):
    b, s, _ = q.shape
    hp = MLA_HEADS // 2
    return pl.pallas_call(
        functools.partial(_mla_prompt_kernel, t=t),
        grid=(b, hp, s // t),
        in_specs=[pl.BlockSpec((1, t, 2 * HEAD_PAD), lambda b_, h, i: (b_, i, h)),
                  pl.BlockSpec((1, s, 2 * HEAD_PAD), lambda b_, h, i: (b_, 0, h)),
                  pl.BlockSpec((1, s, 2 * MLA_V), lambda b_, h, i: (b_, 0, h))],
        out_specs=pl.BlockSpec((1, t, 2 * MLA_V), lambda b_, h, i: (b_, i, h)),
        out_shape=jax.ShapeDtypeStruct((b, s, MLA_HEADS * MLA_V), F32),
        compiler_params=_params(("parallel", "parallel", "arbitrary")), name="mla_prompt",
    )(q, k, v)


def _log_sig_neg(z):
    return -(jnp.maximum(z, 0.0) + jnp.log(1.0 + jnp.exp(-jnp.abs(z))))


def _suffix_tri(n):
    return (lax.broadcasted_iota(jnp.int32, (n, n), 0) >= lax.broadcasted_iota(jnp.int32, (n, n), 1)).astype(BF16)


def _sb_block(qh, kb, vb, tri, carry, allowed):
    z = _dot_nt(qh, kb)
    lsm = _log_sig_neg(z)
    if allowed is not None:
        lsm = jnp.where(allowed, lsm, 0.0)
    hi, lo = _split2(lsm)
    cs = _dot(hi, tri) + _dot(lo, tri)
    la = z + cs + carry
    if allowed is not None:
        la = jnp.where(allowed, la, MASK_VALUE)
    pv = _dot(jnp.exp(la).astype(BF16), vb)
    return pv, carry + cs[:, :1]


def _sb_prompt_kernel(q_ref, k_ref, v_ref, o_ref, *, t):
    qi = pl.program_id(2)
    q = q_ref[0]
    head1 = lax.broadcasted_iota(jnp.int32, (1, LANES), 1) >= SB_DIM
    zq = jnp.zeros_like(q)
    qh = (jnp.where(head1, zq, q), jnp.where(head1, q, zq))
    tri = _suffix_tri(t)
    strict = lax.broadcasted_iota(jnp.int32, (t, t), 1) < lax.broadcasted_iota(jnp.int32, (t, t), 0)

    def block(jb, c0, c1, acc, allowed):
        ks = pl.multiple_of(jb * t, t)
        kb = k_ref[0, pl.ds(ks, t), :].astype(BF16)
        vb = v_ref[0, pl.ds(ks, t), :].astype(BF16)
        pv0, c0 = _sb_block(qh[0], kb, vb, tri, c0, allowed)
        pv1, c1 = _sb_block(qh[1], kb, vb, tri, c1, allowed)
        return c0, c1, acc + jnp.where(head1, pv1, pv0)

    zero = jnp.zeros((t, 1), F32)
    c0, c1, acc = block(qi, zero, zero, jnp.zeros((t, LANES), F32), strict)

    def cond(st):
        return jnp.logical_and(st[0] >= 0, st[4] > SB_EXIT)

    def body(st):
        jb, c0, c1, acc, _ = st
        c0, c1, acc = block(jb, c0, c1, acc, None)
        return jb - 1, c0, c1, acc, jnp.max(jnp.maximum(c0, c1))

    st = lax.while_loop(cond, body, (qi - 1, c0, c1, acc, jnp.max(jnp.maximum(c0, c1))))
    o_ref[0] = st[3]


def _sb_prompt(q, k, v, t):
    b, s, _ = q.shape
    hp = SB_HEADS // 2
    spec_q = pl.BlockSpec((1, t, LANES), lambda b_, h, i: (b_, i, h))
    spec_kv = pl.BlockSpec((1, s, LANES), lambda b_, h, i: (b_, 0, h))
    return pl.pallas_call(
        functools.partial(_sb_prompt_kernel, t=t),
        grid=(b, hp, s // t),
        in_specs=[spec_q, spec_kv, spec_kv],
        out_specs=spec_q,
        out_shape=jax.ShapeDtypeStruct((b, s, SB_WIDTH), F32),
        compiler_params=_params(("parallel", "parallel", "arbitrary")), name="sb_prompt",
    )(q, k, v)


def _mla_sample_kernel(pt_ref, qlat_ref, qr_ref, cnew_ref, rnew_ref, *rest, pg, nt, ng):
    ckv_refs = rest[:pg]
    kr_refs = rest[pg:2 * pg]
    o_ref, kbuf, rbuf, m_scr, l_scr, acc_scr = rest[2 * pg:]
    g = pl.program_id(1)
    ql = qlat_ref[0]
    qr = qr_ref[0]
    nrow = ql.shape[0]

    def update(s, vals):
        m = m_scr[...]
        mn = jnp.maximum(m, jnp.max(s, axis=-1, keepdims=True))
        a = jnp.exp(m - mn)
        p = jnp.exp(s - mn)
        l_scr[...] = a * l_scr[...] + jnp.sum(p, axis=-1, keepdims=True)
        acc_scr[...] = a * acc_scr[...] + _dot(p.astype(BF16), vals)
        m_scr[...] = mn

    @pl.when(g == 0)
    def _():
        m_scr[...] = jnp.full(m_scr.shape, MASK_VALUE, F32)
        l_scr[...] = jnp.zeros(l_scr.shape, F32)
        acc_scr[...] = jnp.zeros(acc_scr.shape, F32)
        cn = cnew_ref[0].astype(BF16)
        s = _dot_nt(ql, cn) + _dot_nt(qr, rnew_ref[0].astype(BF16))
        tq = lax.broadcasted_iota(jnp.int32, s.shape, 0) // MLA_HEADS
        sk = lax.broadcasted_iota(jnp.int32, s.shape, 1)
        s = jnp.where(jnp.logical_and(sk <= tq, sk < nt), s, MASK_VALUE)
        update(s, cn)

    for i in range(pg):
        kbuf[i * PAGE_SIZE:(i + 1) * PAGE_SIZE, :] = ckv_refs[i][...].astype(BF16)
        rbuf[i * PAGE_SIZE:(i + 1) * PAGE_SIZE, :] = kr_refs[i][...].astype(BF16)
    kb = kbuf[...]
    update(_dot_nt(ql, kb) + _dot_nt(qr, rbuf[...]), kb)

    @pl.when(g == ng - 1)
    def _():
        o_ref[0] = acc_scr[...] / l_scr[...]


def _mla_sample(page_table, qlat, qr, cnew, rnew, cache_ckv, cache_kr, layer, pg):
    b, nrow, _ = qlat.shape
    npages = page_table.shape[1]
    ng = npages // pg
    nt = nrow // MLA_HEADS
    pad = PAGE_SIZE - cnew.shape[1]
    cnew = jnp.pad(cnew, ((0, 0), (0, pad), (0, 0)))
    rnew = jnp.pad(rnew, ((0, 0), (0, pad), (0, 0)))

    def page_spec(i, width):
        return pl.BlockSpec((None, None, PAGE_SIZE, width),
                            lambda b_, g, pt: (layer, pt[b_ * npages + g * pg + i], 0, 0))

    seq = lambda r, w: pl.BlockSpec((1, r, w), lambda b_, g, pt: (b_, 0, 0))
    grid_spec = pltpu.PrefetchScalarGridSpec(
        num_scalar_prefetch=1, grid=(b, ng),
        in_specs=[seq(nrow, MLA_KV_LORA), seq(nrow, MLA_ROPE), seq(PAGE_SIZE, MLA_KV_LORA), seq(PAGE_SIZE, MLA_ROPE)]
        + [page_spec(i, MLA_KV_LORA) for i in range(pg)] + [page_spec(i, MLA_ROPE) for i in range(pg)],
        out_specs=seq(nrow, MLA_KV_LORA),
        scratch_shapes=[pltpu.VMEM((pg * PAGE_SIZE, MLA_KV_LORA), BF16), pltpu.VMEM((pg * PAGE_SIZE, MLA_ROPE), BF16),
                        pltpu.VMEM((nrow, 1), F32), pltpu.VMEM((nrow, 1), F32), pltpu.VMEM((nrow, MLA_KV_LORA), F32)])
    return pl.pallas_call(
        functools.partial(_mla_sample_kernel, pg=pg, nt=nt, ng=ng),
        grid_spec=grid_spec,
        out_shape=jax.ShapeDtypeStruct((b, nrow, MLA_KV_LORA), F32),
        compiler_params=_params(("parallel", "arbitrary")), name="mla_sample",
    )(page_table.reshape(-1), qlat, qr, cnew, rnew, *([cache_ckv] * pg), *([cache_kr] * pg))


def _sb_sample_kernel(pt_ref, q_ref, knew_ref, vnew_ref, *rest, pg, nt):
    k_refs = rest[:pg]
    v_refs = rest[pg:2 * pg]
    o_ref, c_scr, acc_scr = rest[2 * pg:]
    g = pl.program_id(1)
    q = q_ref[0]
    nrow = q.shape[0]
    tri = _suffix_tri(PAGE_SIZE)

    def step(kb, vb, allowed):
        pv, c = _sb_block(q, kb.astype(BF16), vb.astype(BF16), tri, c_scr[...], allowed)
        acc_scr[...] += pv
        c_scr[...] = c

    @pl.when(g == 0)
    def _():
        c_scr[...] = jnp.zeros(c_scr.shape, F32)
        acc_scr[...] = jnp.zeros(acc_scr.shape, F32)
        tq = lax.broadcasted_iota(jnp.int32, (nrow, PAGE_SIZE), 0) // SB_HEADS
        sk = lax.broadcasted_iota(jnp.int32, (nrow, PAGE_SIZE), 1)
        step(knew_ref[0], vnew_ref[0], jnp.logical_and(sk < tq, sk < nt))

    for i in reversed(range(pg)):
        step(k_refs[i][...], v_refs[i][...], None)

    @pl.when(g == pl.num_programs(1) - 1)
    def _():
        own = (lax.broadcasted_iota(jnp.int32, (nrow, SB_WIDTH), 1) // SB_DIM
               == lax.broadcasted_iota(jnp.int32, (nrow, SB_WIDTH), 0) % SB_HEADS)
        o_ref[0] = jnp.where(own, acc_scr[...], 0.0)


def _sb_sample(page_table, qm, knew, vnew, cache_k, cache_v, layer, pg):
    b, nrow, _ = qm.shape
    npages = page_table.shape[1]
    ng = npages // pg
    nt = nrow // SB_HEADS
    pad = PAGE_SIZE - knew.shape[1]
    knew = jnp.pad(knew, ((0, 0), (0, pad), (0, 0)))
    vnew = jnp.pad(vnew, ((0, 0), (0, pad), (0, 0)))
    ck = cache_k.reshape(cache_k.shape[:3] + (SB_WIDTH,))
    cv = cache_v.reshape(cache_v.shape[:3] + (SB_WIDTH,))

    def page_spec(i):
        return pl.BlockSpec((None, None, PAGE_SIZE, SB_WIDTH),
                            lambda b_, g, pt: (layer, pt[b_ * npages + (ng - 1 - g) * pg + i], 0, 0))

    seq = lambda r: pl.BlockSpec((1, r, SB_WIDTH), lambda b_, g, pt: (b_, 0, 0))
    grid_spec = pltpu.PrefetchScalarGridSpec(
        num_scalar_prefetch=1, grid=(b, ng),
        in_specs=[seq(nrow), seq(PAGE_SIZE), seq(PAGE_SIZE)] + [page_spec(i) for i in range(pg)] * 2,
        out_specs=seq(nrow),
        scratch_shapes=[pltpu.VMEM((nrow, 1), F32), pltpu.VMEM((nrow, SB_WIDTH), F32)])
    return pl.pallas_call(
        functools.partial(_sb_sample_kernel, pg=pg, nt=nt),
        grid_spec=grid_spec,
        out_shape=jax.ShapeDtypeStruct((b, nrow, SB_WIDTH), F32),
        compiler_params=_params(("parallel", "arbitrary")), name="sb_sample",
    )(page_table.reshape(-1), qm, knew, vnew, *([ck] * pg), *([cv] * pg))


def _mm_kernel(x_ref, w_ref, o_ref):
    o_ref[...] = _dot(x_ref[...].astype(BF16), w_ref[...]).astype(o_ref.dtype)


def _mm(x, w, out_dtype):
    m, n = x.shape[0], w.shape[1]
    return pl.pallas_call(
        _mm_kernel, grid=(1,), in_specs=[_full(x.shape), _full(w.shape)], out_specs=_full((m, n)),
        out_shape=jax.ShapeDtypeStruct((m, n), out_dtype), compiler_params=_params(("arbitrary",)), name="mm",
    )(x, w)


def _post_kernel(x_ref, a_ref, b_ref, wa_ref, wb_ref, g_ref, wu_ref, wd_ref, fg_ref, o_ref,
                 x1_scr, h_scr, acc_scr, *, final):
    j = pl.program_id(1)

    @pl.when(j == 0)
    def _():
        x1 = (x_ref[...] + _dot(a_ref[...].astype(BF16), wa_ref[...])
              + _dot(b_ref[...].astype(BF16), wb_ref[...]))
        x1_scr[...] = x1
        h_scr[...] = _rms(x1, g_ref[...]).astype(BF16)
        acc_scr[...] = jnp.zeros(acc_scr.shape, F32)

    u = jnp.maximum(_dot(h_scr[...], wu_ref[...]), 0.0)
    acc_scr[...] += _dot((u * u).astype(BF16), wd_ref[...])

    @pl.when(j == pl.num_programs(1) - 1)
    def _():
        y = x1_scr[...] + acc_scr[...]
        o_ref[...] = _rms(y, fg_ref[...]) if final else y


def _post(x, a, b, wa, wb, gain, wu, wd, fgain, final, tm, tf):
    m = x.shape[0]
    ff = wu.shape[1]
    row = lambda w: pl.BlockSpec((tm, w), lambda i, j: (i, 0))
    const = lambda shape: pl.BlockSpec(shape, lambda i, j: (0, 0))
    return pl.pallas_call(
        functools.partial(_post_kernel, final=final),
        grid=(m // tm, ff // tf),
        in_specs=[row(D_MODEL), row(a.shape[1]), row(b.shape[1]), const(wa.shape), const(wb.shape),
                  const((1, D_MODEL)), pl.BlockSpec((D_MODEL, tf), lambda i, j: (0, j)),
                  pl.BlockSpec((tf, D_MODEL), lambda i, j: (j, 0)), const((1, D_MODEL))],
        out_specs=row(D_MODEL),
        out_shape=jax.ShapeDtypeStruct((m, D_MODEL), F32),
        scratch_shapes=[pltpu.VMEM((tm, D_MODEL), F32), pltpu.VMEM((tm, D_MODEL), BF16),
                        pltpu.VMEM((tm, D_MODEL), F32)],
        compiler_params=_params(("parallel", "arbitrary")), name="post_mlp",
    )(x, a, b, wa, wb, gain.reshape(1, -1), wu, wd, fgain.reshape(1, -1))


HGW = HG_HEADS * HG_DK
OD_Q, OD_F, OD_I, OD_G = 0, HGW, 2 * HGW, 3 * HGW
OD_DQ = 4 * HGW
OD_DK = OD_DQ + SW_HEADS * SW_DIM
OD_DV = OD_DK + SW_KV_HEADS * SW_DIM
OD_N = OD_DV + SW_KV_HEADS * SW_DIM


def _odd_proj_kernel(x_ref, g_ref, win_ref, lbl_ref, q_out, k_out, v_out, lf_out, z_out, dq_out, dk_out, dv_out,
                     *, layer):
    h = _rms(x_ref[...], g_ref[...]).astype(BF16)
    hh = _dot(h, win_ref[...])
    lbl = lbl_ref[...]
    ex = jnp.exp(lbl - jnp.max(lbl, axis=0, keepdims=True))
    pr = ex / jnp.sum(ex, axis=0, keepdims=True)
    lb = jnp.zeros((1, HGW), F32)
    for i in range(1, layer + 1):
        lb = lb + pr[i:i + 1, :]
    hq = hh[:, OD_Q:OD_F]
    f = lb + (1.0 - lb) * jax.nn.sigmoid(hh[:, OD_F:OD_I])
    fc = jnp.maximum(f, FORGET_FLOOR)
    q_out[...] = hq * jax.nn.sigmoid(hq)
    k_out[...] = 1.0 - fc
    v_out[...] = hh[:, OD_I:OD_G]
    lf_out[...] = jnp.log(fc)
    hz = hh[:, OD_G:OD_DQ]
    z_out[...] = hz * jax.nn.sigmoid(hz)
    dq_out[...] = (hh[:, OD_DQ:OD_DK] * SW_SCALE).astype(BF16)
    dk_out[...] = hh[:, OD_DK:OD_DV]
    dv_out[...] = hh[:, OD_DV:OD_N]


def _odd_proj(x, gain, win, lb_logits, layer, tm):
    m = x.shape[0]
    row = lambda w: pl.BlockSpec((tm, w), lambda i: (i, 0))
    sds = jax.ShapeDtypeStruct
    kvw = SW_KV_HEADS * SW_DIM
    return pl.pallas_call(
        functools.partial(_odd_proj_kernel, layer=layer), grid=(m // tm,),
        in_specs=[row(D_MODEL), _full((1, D_MODEL)), _full(win.shape), _full(lb_logits.shape)],
        out_specs=[row(HGW)] * 5 + [row(SW_HEADS * SW_DIM), row(kvw), row(kvw)],
        out_shape=[sds((m, HGW), F32)] * 5 + [sds((m, SW_HEADS * SW_DIM), BF16), sds((m, kvw), F32), sds((m, kvw), F32)],
        compiler_params=_params(("parallel",)), name="odd_proj",
    )(x, gain.reshape(1, -1), win, lb_logits)


def _hgrn_kernel(q_ref, k_ref, v_ref, g_ref, z_ref, s0_ref, gn_ref, y_ref, sout_ref, gc_scr, st_scr, *, c):
    ci = pl.program_id(2)
    nsub = c // HG_SUB

    @pl.when(ci == 0)
    def _():
        st_scr[...] = s0_ref[0, 0]

    r = lax.broadcasted_iota(jnp.int32, (c, c), 0)
    cc = lax.broadcasted_iota(jnp.int32, (c, c), 1)
    lseg = jnp.logical_and(r >= cc, r // HG_SUB == cc // HG_SUB).astype(BF16)
    g = g_ref[0]
    g1 = g.astype(BF16)
    r1 = g - g1.astype(F32)
    g2 = r1.astype(BF16)
    g3 = (r1 - g2.astype(F32)).astype(BF16)
    gc_scr[...] = _dot(lseg, g1) + _dot(lseg, g2) + _dot(lseg, g3)

    same_head = (lax.broadcasted_iota(jnp.int32, (LANES, LANES), 0) // HG_DV
                 == lax.broadcasted_iota(jnp.int32, (LANES, LANES), 1) // HG_DK)
    ones_bd = same_head.astype(BF16)
    rows = lax.broadcasted_iota(jnp.int32, (HG_SUB, LANES), 0)
    gn = gn_ref[...]

    def sub(i, carry):
        off = pl.multiple_of(i * HG_SUB, HG_SUB)
        q = q_ref[0, pl.ds(off, HG_SUB), :]
        k = k_ref[0, pl.ds(off, HG_SUB), :]
        v = v_ref[0, pl.ds(off, HG_SUB), :]
        z = z_ref[0, pl.ds(off, HG_SUB), :]
        gc = gc_scr[pl.ds(off, HG_SUB), :]
        st = st_scr[...]
        xs = []
        for s in range(HG_SUB):
            d = jnp.exp(jnp.minimum(gc - gc[s:s + 1, :], 0.0))
            xs.append(jnp.where(rows >= s, q * k[s:s + 1, :] * d, 0.0))
        x = jnp.concatenate(xs, axis=0).astype(BF16)
        a = _dot(x, ones_bd)
        o = jnp.zeros((HG_SUB, LANES), F32)
        for s in range(HG_SUB):
            o = o + a[s * HG_SUB:(s + 1) * HG_SUB, :] * v[s:s + 1, :]
        bd = jnp.where(same_head, jnp.concatenate([st, st], axis=0), 0.0).astype(BF16)
        o = o + _dot_nt((q * jnp.exp(gc)).astype(BF16), bd)
        hi, lo = _split2(o * o)
        ms = (_dot(hi, ones_bd) + _dot(lo, ones_bd)) * (1.0 / HG_DV)
        y_ref[0, pl.ds(off, HG_SUB), :] = o * lax.rsqrt(ms + NORM_EPS) * gn * z
        gl = gc[HG_SUB - 1:HG_SUB, :]
        kt = (k * jnp.exp(gl - gc)).astype(BF16)
        full = _dot(v.astype(BF16).T, kt)
        fm = jnp.where(same_head, full, 0.0)
        st_scr[...] = jnp.exp(gl) * st + fm[:HG_DV, :] + fm[HG_DV:, :]
        return carry

    lax.fori_loop(0, nsub, sub, 0)

    @pl.when(ci == pl.num_programs(2) - 1)
    def _():
        sout_ref[0, 0] = st_scr[...]


def _hgrn(q, k, v, g, z, s0, gnorm, c):
    b, t, _ = q.shape
    hp = HG_HEADS // 2
    tok = pl.BlockSpec((1, c, LANES), lambda b_, h, i: (b_, i, h))
    stt = pl.BlockSpec((1, 1, HG_DV, LANES), lambda b_, h, i: (b_, h, 0, 0))
    gn = jnp.tile(gnorm.reshape(1, -1), (1, LANES // HG_DV))
    return pl.pallas_call(
        functools.partial(_hgrn_kernel, c=c),
        grid=(b, hp, t // c),
        in_specs=[tok] * 5 + [stt, pl.BlockSpec((1, LANES), lambda b_, h, i: (0, 0))],
        out_specs=[tok, stt],
        out_shape=[jax.ShapeDtypeStruct((b, t, HGW), F32), jax.ShapeDtypeStruct((b, hp, HG_DV, LANES), F32)],
        scratch_shapes=[pltpu.VMEM((c, LANES), F32), pltpu.VMEM((HG_DV, LANES), F32)],
        compiler_params=_params(("parallel", "parallel", "arbitrary")), name="hgrn2",
    )(q, k, v, g, z, s0, gn)


def _state_to_t(s):
    b = s.shape[0]
    return s.reshape(b, HG_HEADS // 2, 2, HG_DK, HG_DV).transpose(0, 1, 4, 2, 3).reshape(b, HG_HEADS // 2, HG_DV, 2 * HG_DK)


def _state_from_t(st):
    b = st.shape[0]
    return st.reshape(b, HG_HEADS // 2, HG_DV, 2, HG_DK).transpose(0, 1, 3, 4, 2).reshape(b, HG_HEADS, HG_DK, HG_DV)


def _rel_bucket_np(rel):
    n = np.maximum(rel, 0)
    exact = REL_BUCKETS // 2
    scaled = np.log(np.maximum(n, 1).astype(np.float32) / exact) / math.log(REL_MAX_DIST / exact)
    large = np.minimum(exact + (np.maximum(scaled, 0.0) * (REL_BUCKETS - exact)).astype(np.int32), REL_BUCKETS - 1)
    return np.where(n < exact, n, large).astype(np.int32)


def _swa_kernel(relb_ref, sink_ref, bucket_ref, q_ref, kp_ref, kc_ref, vp_ref, vc_ref, o_ref, bias_scr,
                *, first_has_no_prev):
    bi = pl.program_id(1)
    w = WINDOW

    @pl.when(jnp.logical_and(pl.program_id(0) == 0, bi == 0))
    def _():
        bucket = bucket_ref[...]
        for h in range(SW_HEADS):
            acc = jnp.zeros((w, 2 * w), F32)
            for bk in range(REL_BUCKETS):
                acc = jnp.where(bucket == bk, relb_ref[bk, h], acc)
            bias_scr[h] = acc

    qi = lax.broadcasted_iota(jnp.int32, (w, 2 * w), 0)
    kj = lax.broadcasted_iota(jnp.int32, (w, 2 * w), 1)
    rel = qi + w - kj
    allowed = jnp.logical_and(rel >= 0, rel <= WINDOW)
    if first_has_no_prev:
        allowed = jnp.logical_and(allowed, jnp.logical_or(bi > 0, kj >= w))
    lane_hi = lax.broadcasted_iota(jnp.int32, (1, LANES), 1) >= SW_DIM
    k = jnp.concatenate([kp_ref[0], kc_ref[0]], axis=0)
    v = jnp.concatenate([vp_ref[0], vc_ref[0]], axis=0)
    ksw = pltpu.roll(k, SW_DIM, 1)
    vsw = pltpu.roll(v, SW_DIM, 1)
    kg = (jnp.where(lane_hi, ksw, k).astype(BF16), jnp.where(lane_hi, k, ksw).astype(BF16))
    vg = (jnp.where(lane_hi, vsw, v).astype(BF16), jnp.where(lane_hi, v, vsw).astype(BF16))
    q = q_ref[0]
    outs = []
    for pair in range(SW_HEADS // 2):
        qb = q[:, pair * LANES:(pair + 1) * LANES]
        zq = jnp.zeros_like(qb)
        res = []
        for half in range(2):
            h = 2 * pair + half
            g = h // SW_GROUP
            qm = jnp.where(lane_hi, qb, zq) if half else jnp.where(lane_hi, zq, qb)
            s = jnp.where(allowed, _dot_nt(qm, kg[g]) + bias_scr[h], MASK_VALUE)
            sink = sink_ref[h]
            m = jnp.maximum(jnp.max(s, axis=-1, keepdims=True), sink)
            e = jnp.exp(s - m)
            den = jnp.sum(e, axis=-1, keepdims=True) + jnp.exp(sink - m)
            res.append(_dot((e / den).astype(BF16), vg[g]))
        outs.append(jnp.where(lane_hi, res[1], res[0]))
    o_ref[0] = jnp.concatenate(outs, axis=1)


def _swa(q, kprev_src, kcur_src, vprev_src, vcur_src, rel_bias, sinks, prev_is_shifted):
    b, s, _ = q.shape
    w = WINDOW
    nb = s // w
    kvw = SW_KV_HEADS * SW_DIM
    rel = (np.arange(w)[:, None] + w) - np.arange(2 * w)[None, :]
    bucket = jnp.asarray(_rel_bucket_np(rel))
    cur = pl.BlockSpec((1, w, kvw), lambda b_, i: (b_, i, 0))
    prev = pl.BlockSpec((1, w, kvw), lambda b_, i: (b_, jnp.maximum(i - 1, 0), 0)) if prev_is_shifted else cur
    smem = pl.BlockSpec(memory_space=pltpu.SMEM)
    return pl.pallas_call(
        functools.partial(_swa_kernel, first_has_no_prev=prev_is_shifted),
        grid=(b, nb),
        in_specs=[smem, smem, pl.BlockSpec((w, 2 * w), lambda b_, i: (0, 0)),
                  pl.BlockSpec((1, w, SW_HEADS * SW_DIM), lambda b_, i: (b_, i, 0)), prev, cur, prev, cur],
        out_specs=pl.BlockSpec((1, w, SW_HEADS * SW_DIM), lambda b_, i: (b_, i, 0)),
        out_shape=jax.ShapeDtypeStruct((b, s, SW_HEADS * SW_DIM), F32),
        scratch_shapes=[pltpu.VMEM((SW_HEADS, w, 2 * w), F32)],
        compiler_params=_params(("arbitrary", "arbitrary")), name="swa",
    )(rel_bias, sinks, bucket, q, kprev_src, kcur_src, vprev_src, vcur_src)


def _tile(m, pref):
    return pref if m % pref == 0 else m


def kernel(x_prompt, x_sample, cache_mla_ckv, cache_mla_krope, cache_sb_k, cache_sb_v, state_hgrn, state_swa_k, state_swa_v, page_table, mix_norm, w_in_even, mla_q_norm, mla_w_qb, mla_kv_norm, mla_w_kvb, w_out_even, w_in_odd, hgrn_lb_logits, hgrn_out_norm, swa_sinks, w_out_odd, rel_bias, mlp_norm, w_up, w_down, final_norm):
    bp, sp, d = x_prompt.shape
    bs, ts, _ = x_sample.shape
    depth = mix_norm.shape[0]
    past = page_table.shape[1] * PAGE_SIZE
    mp, ms = bp * sp, bs * ts
    xp = x_prompt.reshape(mp, d)
    xs = x_sample.reshape(ms, d)
    tab_p = jnp.tile(_rope_tables(jnp.arange(sp)), (bp, 1))
    tab_s = jnp.tile(_rope_tables(past + jnp.arange(ts)), (bs, 1))
    tm_p, tm_s = _tile(mp, 256), _tile(ms, 256)
    t_att = _tile(sp, 256)
    pg = 8 if page_table.shape[1] % 8 == 0 else 1
    nbuf = state_swa_k.shape[2]
    hg_pad = HG_SUB - ts
    kvw = SW_KV_HEADS * SW_DIM
    own_head = (jnp.arange(SB_WIDTH)[None, :] // SB_DIM == jnp.arange(SB_HEADS)[:, None])

    outs_p = {k: [] for k in ("ckv", "kr", "sbk", "sbv", "hg", "swk", "swv")}
    outs_s = {k: [] for k in ("ckv", "kr", "sbk", "sbv", "hg", "swk", "swv")}
    for layer in range(depth):
        last = layer == depth - 1
        if layer % 2 == 0:
            e = layer // 2
            wts = _even_weights(w_in_even[e], mla_w_qb[e], mla_w_kvb[e])
            wvbd = wts[6]
            q_p, k_p, v_p, c_p, r_p, sq_p, sk_p, sv_p = _even_proj(
                xp, mix_norm[layer], wts, mla_q_norm[e], mla_kv_norm[e], tab_p, False, tm_p)
            q_s, ql_s, c_s, r_s, sq_s, sk_s, sv_s = _even_proj(
                xs, mix_norm[layer], wts, mla_q_norm[e], mla_kv_norm[e], tab_s, True, tm_s)
            a_p = _mla_prompt(q_p.reshape(bp, sp, -1), k_p.reshape(bp, sp, -1), v_p.reshape(bp, sp, -1), t_att)
            b_p = _sb_prompt(sq_p.reshape(bp, sp, -1), sk_p.reshape(bp, sp, -1), sv_p.reshape(bp, sp, -1), t_att)
            nrow = ts * MLA_HEADS
            qlat = ql_s.reshape(bs, nrow, MLA_KV_LORA)
            qrope = q_s.reshape(bs, nrow, HEAD_PAD)[:, :, MLA_NOPE:MLA_NOPE + MLA_ROPE]
            olat = _mla_sample(page_table, qlat, qrope, c_s.reshape(bs, ts, -1), r_s.reshape(bs, ts, -1),
                               cache_mla_ckv, cache_mla_krope, e, pg)
            a_s = _mm(olat.reshape(ms, MLA_HEADS * MLA_KV_LORA), wvbd, F32)
            qm = jnp.where(own_head[None, None], sq_s.reshape(bs, ts, 1, SB_WIDTH), jnp.zeros((), BF16))
            o_sb = _sb_sample(page_table, qm.reshape(bs, nrow, SB_WIDTH), sk_s.reshape(bs, ts, -1),
                              sv_s.reshape(bs, ts, -1), cache_sb_k, cache_sb_v, e, pg)
            b_s = o_sb.reshape(bs, ts, SB_HEADS, SB_WIDTH).sum(axis=2).reshape(ms, SB_WIDTH)
            a_p, b_p = a_p.reshape(mp, -1), b_p.reshape(mp, -1)
            w_out = w_out_even[e].astype(BF16)
            wa, wb = w_out[:MLA_HEADS * MLA_V], w_out[MLA_HEADS * MLA_V:]
            outs_p["ckv"].append(c_p.reshape(bp, sp, -1)); outs_p["kr"].append(r_p.reshape(bp, sp, -1))
            outs_p["sbk"].append(sk_p.reshape(bp, sp, SB_HEADS, SB_DIM)); outs_p["sbv"].append(sv_p.reshape(bp, sp, SB_HEADS, SB_DIM))
            outs_s["ckv"].append(c_s.reshape(bs, ts, -1)); outs_s["kr"].append(r_s.reshape(bs, ts, -1))
            outs_s["sbk"].append(sk_s.reshape(bs, ts, SB_HEADS, SB_DIM)); outs_s["sbv"].append(sv_s.reshape(bs, ts, SB_HEADS, SB_DIM))
        else:
            o = layer // 2
            win = w_in_odd[o].astype(BF16)
            lbl = hgrn_lb_logits.astype(F32)
            hq_p, hk_p, hv_p, hg_p, hz_p, dq_p, dk_p, dv_p = _odd_proj(xp, mix_norm[layer], win, lbl, o, tm_p)
            hq_s, hk_s, hv_s, hg_s, hz_s, dq_s, dk_s, dv_s = _odd_proj(xs, mix_norm[layer], win, lbl, o, tm_s)
            r3 = lambda a, b_: a.reshape(b_, -1, a.shape[-1])
            s0_p = jnp.zeros((bp, HG_HEADS // 2, HG_DV, LANES), F32)
            a_p, st_p = _hgrn(r3(hq_p, bp), r3(hk_p, bp), r3(hv_p, bp), r3(hg_p, bp), r3(hz_p, bp), s0_p,
                              hgrn_out_norm[o], _tile(sp, 256))
            padt = lambda a: jnp.pad(r3(a, bs), ((0, 0), (0, hg_pad), (0, 0)))
            a_s, st_s = _hgrn(padt(hq_s), padt(hk_s), padt(hv_s), padt(hg_s), padt(hz_s),
                              _state_to_t(state_hgrn[o].astype(F32)), hgrn_out_norm[o], HG_SUB)
            a_s = a_s[:, :ts].reshape(ms, -1)
            dk3, dv3 = r3(dk_p, bp), r3(dv_p, bp)
            b_p = _swa(r3(dq_p, bp), dk3, dk3, dv3, dv3, rel_bias, swa_sinks[o], True)
            padw = lambda a: jnp.pad(r3(a, bs), ((0, 0), (0, WINDOW - ts), (0, 0)))
            kbuf = state_swa_k[o].reshape(bs, nbuf, kvw)
            vbuf = state_swa_v[o].reshape(bs, nbuf, kvw)
            b_s = _swa(padw(dq_s), kbuf, padw(dk_s), vbuf, padw(dv_s), rel_bias, swa_sinks[o], False)
            b_s = b_s[:, :ts].reshape(ms, -1)
            a_p, b_p = a_p.reshape(mp, -1), b_p.reshape(mp, -1)
            w_out = w_out_odd[o].astype(BF16)
            wa, wb = w_out[:HG_HEADS * HG_DV], w_out[HG_HEADS * HG_DV:]
            nkeep = min(WINDOW, sp)
            outs_p["hg"].append(_state_from_t(st_p)); outs_s["hg"].append(_state_from_t(st_s))
            outs_p["swk"].append(dk3[:, -nkeep:].reshape(bp, nkeep, SW_KV_HEADS, SW_DIM))
            outs_p["swv"].append(dv3[:, -nkeep:].reshape(bp, nkeep, SW_KV_HEADS, SW_DIM))
            kk = jnp.concatenate([kbuf, r3(dk_s, bs)], axis=1)[:, -nbuf:]
            vv = jnp.concatenate([vbuf, r3(dv_s, bs)], axis=1)[:, -nbuf:]
            outs_s["swk"].append(kk.reshape(bs, nbuf, SW_KV_HEADS, SW_DIM))
            outs_s["swv"].append(vv.reshape(bs, nbuf, SW_KV_HEADS, SW_DIM))
        wu, wd = w_up[layer].astype(BF16), w_down[layer].astype(BF16)
        xp = _post(xp, a_p, b_p, wa, wb, mlp_norm[layer], wu, wd, final_norm, last, _tile(mp, 512), _tile(D_FF, 1024))
        xs = _post(xs, a_s, b_s, wa, wb, mlp_norm[layer], wu, wd, final_norm, last, _tile(ms, 512), _tile(D_FF, 1024))
    st = lambda lst: jnp.stack(lst, axis=0)
    return (xp.reshape(bp, sp, d), xs.reshape(bs, ts, d),
            st(outs_p["ckv"]), st(outs_p["kr"]), st(outs_p["sbk"]), st(outs_p["sbv"]), st(outs_p["hg"]),
            st(outs_p["swk"]), st(outs_p["swv"]),
            st(outs_s["ckv"]), st(outs_s["kr"]), st(outs_s["sbk"]), st(outs_s["sbv"]), st(outs_s["hg"]),
            st(outs_s["swk"]), st(outs_s["swv"]))
```

```python
import functools
import math

import numpy as np
import jax
import jax.numpy as jnp
from jax import lax
from jax.experimental import pallas as pl
from jax.experimental.pallas import tpu as pltpu

F32 = jnp.float32
BF16 = jnp.bfloat16

D_MODEL = 1024
PAGE_SIZE = 128
MLA_HEADS = 8
MLA_NOPE = 64
MLA_ROPE = 32
MLA_V = 64
MLA_Q_LORA = 384
MLA_KV_LORA = 256
MLA_SCALE = (MLA_NOPE + MLA_ROPE) ** -0.5
MLA_QSCALE = MLA_SCALE * math.log2(math.e)
ROPE_THETA = 10000.0
SB_HEADS = 8
SB_DIM = 64
SB_WIDTH = SB_HEADS * SB_DIM
SB_SCALE = SB_DIM ** -0.5
HG_HEADS = 8
HG_DK = 64
HG_DV = 64
HG_SUB = 16
FORGET_FLOOR = 1e-30
SW_HEADS = 8
SW_KV_HEADS = 2
SW_GROUP = SW_HEADS // SW_KV_HEADS
SW_DIM = 64
SW_SCALE = SW_DIM ** -0.5
WINDOW = 128
REL_BUCKETS = 32
REL_MAX_DIST = 128
D_FF = 4 * D_MODEL
NORM_EPS = 1e-6
MASK_VALUE = -1e30
SB_EXIT = -150.0

LANES = 128
HEAD_PAD = 128
VMEM_LIMIT = 56 * 1024 * 1024

NT = (((1,), (1,)), ((), ()))


def _dot(a, b):
    return jnp.dot(a, b, preferred_element_type=F32)


def _dot_nt(a, b):
    return lax.dot_general(a, b, NT, preferred_element_type=F32)


def _rms(x, g):
    return x * lax.rsqrt(jnp.mean(x * x, axis=-1, keepdims=True) + NORM_EPS) * g


def _split2(x):
    hi = x.astype(BF16)
    lo = (x - hi.astype(F32)).astype(BF16)
    return hi, lo


def _params(sem):
    return pltpu.CompilerParams(dimension_semantics=sem, vmem_limit_bytes=VMEM_LIMIT)


def _full(shape):
    n = len(shape)
    return pl.BlockSpec(shape, lambda *_: (0,) * n)


EV_QA = 0
EV_KVA = MLA_Q_LORA
EV_SQ = EV_KVA + MLA_KV_LORA
EV_SK = EV_SQ + SB_WIDTH
EV_SV = EV_SK + SB_WIDTH
EV_KR = EV_SV + SB_WIDTH
EV_KRS = EV_KR + LANES
EV_N = EV_KRS + LANES


def _even_common(x_ref, g_ref, win_ref, qn_ref, wq_ref, wqs_ref, kvn_ref, tab_ref):
    h = _rms(x_ref[...], g_ref[...]).astype(BF16)
    hh = _dot(h, win_ref[...])
    tab = tab_ref[...]
    cos_q, sin_q = tab[:, 0:LANES], tab[:, LANES:2 * LANES]
    cos_k, sin_k = tab[:, 2 * LANES:3 * LANES], tab[:, 3 * LANES:4 * LANES]
    qa = _rms(hh[:, EV_QA:EV_KVA], qn_ref[...]).astype(BF16)
    q = (_dot(qa, wq_ref[...]) * jnp.tile(cos_q, (1, MLA_HEADS))
         + _dot(qa, wqs_ref[...]) * jnp.tile(sin_q, (1, MLA_HEADS)))
    ckv = _rms(hh[:, EV_KVA:EV_SQ], kvn_ref[...])
    kr = hh[:, EV_KR:EV_KRS] * cos_k + hh[:, EV_KRS:EV_N] * sin_k
    sq = (hh[:, EV_SQ:EV_SK] * SB_SCALE).astype(BF16)
    return q, ckv, kr, sq, hh[:, EV_SK:EV_SV], hh[:, EV_SV:EV_KR]


def _even_prompt_kernel(x_ref, g_ref, win_ref, qn_ref, wq_ref, wqs_ref, kvn_ref, tab_ref,
                        wk_ref, wv_ref, e_ref,
                        q_out, k_out, v_out, ckv_out, kr_out, sq_out, sk_out, sv_out):
    q, ckv, kr, sq, sk, sv = _even_common(x_ref, g_ref, win_ref, qn_ref, wq_ref, wqs_ref, kvn_ref, tab_ref)
    q_out[...] = q.astype(BF16)
    ckv_out[...] = ckv
    kr_out[...] = kr[:, :MLA_ROPE]
    sq_out[...] = sq
    sk_out[...] = sk
    sv_out[...] = sv
    cb = ckv.astype(BF16)
    k_out[...] = (_dot(cb, wk_ref[...]) + _dot(kr.astype(BF16), e_ref[...])).astype(BF16)
    lane = lax.broadcasted_iota(jnp.int32, (1, MLA_HEADS * HEAD_PAD), 1)
    ones_lane = jnp.where(lane % HEAD_PAD == MLA_V, 1.0, 0.0)
    v_out[...] = (_dot(cb, wv_ref[...]) + ones_lane).astype(BF16)


def _even_sample_kernel(x_ref, g_ref, win_ref, qn_ref, wq_ref, wqs_ref, kvn_ref, tab_ref,
                        wabs_ref,
                        q_out, qlat_out, ckv_out, kr_out, sq_out, sk_out, sv_out):
    q, ckv, kr, sq, sk, sv = _even_common(x_ref, g_ref, win_ref, qn_ref, wq_ref, wqs_ref, kvn_ref, tab_ref)
    qb = q.astype(BF16)
    q_out[...] = qb
    qlat_out[...] = _dot(qb, wabs_ref[...]).astype(BF16)
    ckv_out[...] = ckv
    kr_out[...] = kr[:, :MLA_ROPE]
    sq_out[...] = sq
    sk_out[...] = sk
    sv_out[...] = sv


def _rope_tables(pos):
    half = MLA_ROPE // 2
    inv = ROPE_THETA ** (-jnp.arange(half, dtype=F32) / half)
    ang = pos.astype(F32)[:, None] * inv[None, :]
    cos = jnp.cos(ang)
    sin = jnp.sin(ang)
    cos2 = jnp.concatenate([cos, cos], axis=-1)
    sin2 = jnp.concatenate([sin, sin], axis=-1)
    n = pos.shape[0]
    one = jnp.ones((n, MLA_NOPE), F32)
    z = lambda w: jnp.zeros((n, w), F32)
    cos_q = jnp.concatenate([one, cos2, z(HEAD_PAD - MLA_NOPE - MLA_ROPE)], -1) * MLA_QSCALE
    sin_q = jnp.concatenate([z(MLA_NOPE), sin2, z(HEAD_PAD - MLA_NOPE - MLA_ROPE)], -1) * MLA_QSCALE
    cos_k = jnp.concatenate([cos2, z(LANES - MLA_ROPE)], -1)
    sin_k = jnp.concatenate([sin2, z(LANES - MLA_ROPE)], -1)
    return jnp.concatenate([cos_q, sin_q, cos_k, sin_k], -1)


def _rot_cols(w):
    half = w.shape[-1] // 2
    return jnp.concatenate([-w[..., half:], w[..., :half]], axis=-1)


def _even_weights(w_in, w_qb, w_kvb):
    d = w_in.shape[0]
    o_kva, o_kr = MLA_Q_LORA, MLA_Q_LORA + MLA_KV_LORA
    o_sq = o_kr + MLA_ROPE
    w_kr = w_in[:, o_kr:o_sq]
    zpad = jnp.zeros((d, LANES - MLA_ROPE), F32)
    win = jnp.concatenate([w_in[:, :o_kr], w_in[:, o_sq:], w_kr, zpad, _rot_cols(w_kr), zpad], -1).astype(BF16)
    wq3 = w_qb.reshape(MLA_Q_LORA, MLA_HEADS, MLA_NOPE + MLA_ROPE)
    nope, rp = wq3[..., :MLA_NOPE], wq3[..., MLA_NOPE:]
    zq = lambda w: jnp.zeros((MLA_Q_LORA, MLA_HEADS, w), F32)
    pad = HEAD_PAD - MLA_NOPE - MLA_ROPE
    wq = jnp.concatenate([nope, rp, zq(pad)], -1).reshape(MLA_Q_LORA, -1).astype(BF16)
    wqs = jnp.concatenate([zq(MLA_NOPE), _rot_cols(rp), zq(pad)], -1).reshape(MLA_Q_LORA, -1).astype(BF16)
    wkv3 = w_kvb.reshape(MLA_KV_LORA, MLA_HEADS, MLA_NOPE + MLA_V)
    knope, wv3 = wkv3[..., :MLA_NOPE], wkv3[..., MLA_NOPE:]
    wk = jnp.concatenate([knope, jnp.zeros((MLA_KV_LORA, MLA_HEADS, HEAD_PAD - MLA_NOPE), F32)], -1)
    wk = wk.reshape(MLA_KV_LORA, -1).astype(BF16)
    wv = jnp.concatenate([wv3, jnp.zeros((MLA_KV_LORA, MLA_HEADS, HEAD_PAD - MLA_V), F32)], -1)
    wv = wv.reshape(MLA_KV_LORA, -1).astype(BF16)
    eye = jnp.eye(MLA_HEADS, dtype=F32)
    kt = jnp.concatenate([knope, jnp.zeros((MLA_KV_LORA, MLA_HEADS, HEAD_PAD - MLA_NOPE), F32)], -1)
    wabs = jnp.einsum('chn,hg->hngc', kt, eye).reshape(MLA_HEADS * HEAD_PAD, MLA_HEADS * MLA_KV_LORA).astype(BF16)
    wvbd = jnp.einsum('chv,hg->hcgv', wv3, eye).reshape(MLA_HEADS * MLA_KV_LORA, MLA_HEADS * MLA_V).astype(BF16)
    return win, wq, wqs, wk, wv, wabs, wvbd


def _krope_place():
    e = np.zeros((LANES, MLA_HEADS * HEAD_PAD), np.float32)
    for h in range(MLA_HEADS):
        for i in range(MLA_ROPE):
            e[i, h * HEAD_PAD + MLA_NOPE + i] = 1.0
    return jnp.asarray(e, BF16)


def _even_proj(x, gain, weights, q_norm, kv_norm, tab, sample, tm):
    m = x.shape[0]
    win, wq, wqs, wk, wv, wabs, _ = weights
    hq = MLA_HEADS * HEAD_PAD
    row = lambda w: pl.BlockSpec((tm, w), lambda i: (i, 0))
    common_in = [row(D_MODEL), _full((1, D_MODEL)), _full(win.shape), _full((1, MLA_Q_LORA)), _full(wq.shape),
                 _full(wqs.shape), _full((1, MLA_KV_LORA)), row(4 * LANES)]
    common_args = [x, gain.reshape(1, -1), win, q_norm.reshape(1, -1), wq, wqs, kv_norm.reshape(1, -1), tab]
    sds = jax.ShapeDtypeStruct
    tail_shapes = [sds((m, MLA_KV_LORA), F32), sds((m, MLA_ROPE), F32), sds((m, SB_WIDTH), BF16),
                   sds((m, SB_WIDTH), F32), sds((m, SB_WIDTH), F32)]
    tail_specs = [row(MLA_KV_LORA), row(MLA_ROPE), row(SB_WIDTH), row(SB_WIDTH), row(SB_WIDTH)]
    if sample:
        return pl.pallas_call(
            _even_sample_kernel, grid=(m // tm,),
            in_specs=common_in + [_full(wabs.shape)],
            out_specs=[row(hq), row(MLA_HEADS * MLA_KV_LORA)] + tail_specs,
            out_shape=[sds((m, hq), BF16), sds((m, MLA_HEADS * MLA_KV_LORA), BF16)] + tail_shapes,
            compiler_params=_params(("parallel",)), name="even_proj_sample",
        )(*common_args, wabs)
    e = _krope_place()
    return pl.pallas_call(
        _even_prompt_kernel, grid=(m // tm,),
        in_specs=common_in + [_full(wk.shape), _full(wv.shape), _full(e.shape)],
        out_specs=[row(hq), row(hq), row(hq)] + tail_specs,
        out_shape=[sds((m, hq), BF16), sds((m, hq), BF16), sds((m, hq), BF16)] + tail_shapes,
        compiler_params=_params(("parallel",)), name="even_proj_prompt",
    )(*common_args, wk, wv, e)


def _mla_prompt_kernel(q_ref, k_ref, v_ref, o_ref, *, t, rt):
    qi = pl.program_id(2)
    nsub = t // rt
    qs = [q_ref[0, r * rt:(r + 1) * rt, :] for r in range(nsub)]
    rows = lax.broadcasted_iota(jnp.int32, (rt, t), 0)
    cols = lax.broadcasted_iota(jnp.int32, (rt, t), 1)

    def block(jb, carry, diag):
        ks = pl.multiple_of(jb * t, t)
        k = k_ref[0, pl.ds(ks, t), :]
        v = v_ref[0, pl.ds(ks, t), :]
        ss = [_dot_nt(qs[r], k) for r in range(nsub)]
        if diag:
            ss = [jnp.where(cols <= rows + r * rt, ss[r], MASK_VALUE) for r in range(nsub)]
        mns = [jnp.maximum(carry[2 * r], jnp.max(ss[r], axis=-1, keepdims=True)) for r in range(nsub)]
        ps = [jnp.exp2(ss[r] - mns[r]).astype(BF16) for r in range(nsub)]
        out = []
        for r in range(nsub):
            out += [mns[r], jnp.exp2(carry[2 * r] - mns[r]) * carry[2 * r + 1] + _dot(ps[r], v)]
        return tuple(out)

    init = (jnp.full((rt, 1), MASK_VALUE, F32), jnp.zeros((rt, HEAD_PAD), F32)) * nsub
    carry = lax.fori_loop(0, qi, lambda jb, c: block(jb, c, False), init)
    carry = block(qi, carry, True)
    for r in range(nsub):
        acc = carry[2 * r + 1]
        o_ref[0, r * rt:(r + 1) * rt, :] = acc / acc[:, MLA_V:MLA_V + 1]


def _mla_prompt(q, k, v, t):
    b, s, _ = q.shape
    tile = pl.BlockSpec((1, t, HEAD_PAD), lambda b_, h, i: (b_, i, h))
    seq = pl.BlockSpec((1, s, HEAD_PAD), lambda b_, h, i: (b_, 0, h))
    return pl.pallas_call(
        functools.partial(_mla_prompt_kernel, t=t, rt=min(t, 128)),
        grid=(b, MLA_HEADS, s // t),
        in_specs=[tile, seq, seq],
        out_specs=tile,
        out_shape=jax.ShapeDtypeStruct((b, s, MLA_HEADS * HEAD_PAD), F32),
        compiler_params=_params(("parallel", "parallel", "arbitrary")), name="mla_prompt",
    )(q, k, v)


def _log_sig_neg(z):
    return -(jnp.maximum(z, 0.0) + jnp.log(1.0 + jnp.exp(-jnp.abs(z))))


def _suffix_tri(n):
    return (lax.broadcasted_iota(jnp.int32, (n, n), 0) >= lax.broadcasted_iota(jnp.int32, (n, n), 1)).astype(BF16)


def _sb_block(qh, kb, vb, tri, carry, allowed, key_minor=False):
    z = _dot(qh, kb) if key_minor else _dot_nt(qh, kb)
    lsm = _log_sig_neg(z)
    if allowed is not None:
        lsm = jnp.where(allowed, lsm, 0.0)
    hi, lo = _split2(lsm)
    cs = _dot(hi, tri) + _dot(lo, tri)
    la = z + cs + carry
    if allowed is not None:
        la = jnp.where(allowed, la, MASK_VALUE)
    a = jnp.exp(la).astype(BF16)
    pv = _dot_nt(a, vb) if key_minor else _dot(a, vb)
    return pv, carry + cs[:, :1]


def _sb_prompt_kernel(q_ref, k_ref, v_ref, o_ref, *, t):
    qi = pl.program_id(2)
    q = q_ref[0]
    head1 = lax.broadcasted_iota(jnp.int32, (1, LANES), 1) >= SB_DIM
    zq = jnp.zeros_like(q)
    qh = (jnp.where(head1, zq, q), jnp.where(head1, q, zq))
    tri = _suffix_tri(t)
    strict = lax.broadcasted_iota(jnp.int32, (t, t), 1) < lax.broadcasted_iota(jnp.int32, (t, t), 0)

    def block(jb, c0, c1, acc, allowed):
        ks = pl.multiple_of(jb * t, t)
        kb = k_ref[0, pl.ds(ks, t), :].astype(BF16)
        vb = v_ref[0, pl.ds(ks, t), :].astype(BF16)
        pv0, c0 = _sb_block(qh[0], kb, vb, tri, c0, allowed)
        pv1, c1 = _sb_block(qh[1], kb, vb, tri, c1, allowed)
        return c0, c1, acc + jnp.where(head1, pv1, pv0)

    zero = jnp.zeros((t, 1), F32)
    c0, c1, acc = block(qi, zero, zero, jnp.zeros((t, LANES), F32), strict)

    def cond(st):
        return jnp.logical_and(st[0] >= 0, st[4] > SB_EXIT)

    def body(st):
        jb, c0, c1, acc, _ = st
        c0, c1, acc = block(jb, c0, c1, acc, None)
        return jb - 1, c0, c1, acc, jnp.max(jnp.maximum(c0, c1))

    st = lax.while_loop(cond, body, (qi - 1, c0, c1, acc, jnp.max(jnp.maximum(c0, c1))))
    o_ref[0] = st[3]


def _sb_prompt(q, k, v, t):
    b, s, _ = q.shape
    hp = SB_HEADS // 2
    spec_q = pl.BlockSpec((1, t, LANES), lambda b_, h, i: (b_, i, h))
    spec_kv = pl.BlockSpec((1, s, LANES), lambda b_, h, i: (b_, 0, h))
    return pl.pallas_call(
        functools.partial(_sb_prompt_kernel, t=t),
        grid=(b, hp, s // t),
        in_specs=[spec_q, spec_kv, spec_kv],
        out_specs=spec_q,
        out_shape=jax.ShapeDtypeStruct((b, s, SB_WIDTH), F32),
        compiler_params=_params(("parallel", "parallel", "arbitrary")), name="sb_prompt",
    )(q, k, v)


def _mla_sample_kernel(pt_ref, qlat_ref, qr_ref, cnew_ref, rnew_ref, *rest, pg, nt, ng):
    ckv_refs = rest[:pg]
    kr_refs = rest[pg:2 * pg]
    o_ref, kbuf, rbuf, m_scr, l_scr, acc_scr = rest[2 * pg:]
    g = pl.program_id(1)
    ql = qlat_ref[0]
    qr = qr_ref[0]
    nrow = ql.shape[0]

    def update(s, vals):
        m = m_scr[...]
        mn = jnp.maximum(m, jnp.max(s, axis=-1, keepdims=True))
        a = jnp.exp2(m - mn)
        p = jnp.exp2(s - mn)
        l_scr[...] = a * l_scr[...] + jnp.sum(p, axis=-1, keepdims=True)
        acc_scr[...] = a * acc_scr[...] + _dot(p.astype(BF16), vals)
        m_scr[...] = mn

    @pl.when(g == 0)
    def _():
        m_scr[...] = jnp.full(m_scr.shape, MASK_VALUE, F32)
        l_scr[...] = jnp.zeros(l_scr.shape, F32)
        acc_scr[...] = jnp.zeros(acc_scr.shape, F32)
        cn = cnew_ref[0].astype(BF16)
        s = _dot_nt(ql, cn) + _dot_nt(qr, rnew_ref[0].astype(BF16))
        tq = lax.broadcasted_iota(jnp.int32, s.shape, 0) // MLA_HEADS
        sk = lax.broadcasted_iota(jnp.int32, s.shape, 1)
        s = jnp.where(jnp.logical_and(sk <= tq, sk < nt), s, MASK_VALUE)
        update(s, cn)

    for i in range(pg):
        kbuf[i * PAGE_SIZE:(i + 1) * PAGE_SIZE, :] = ckv_refs[i][...].astype(BF16)
        rbuf[:, i * PAGE_SIZE:(i + 1) * PAGE_SIZE] = kr_refs[i][...].astype(BF16)
    kb = kbuf[...]
    update(_dot_nt(ql, kb) + _dot(qr, rbuf[...]), kb)

    @pl.when(g == ng - 1)
    def _():
        o_ref[0] = acc_scr[...] / l_scr[...]


def _mla_sample(page_table, qlat, qr, cnew, rnew, cache_ckv, cache_kr, layer, pg):
    b, nrow, _ = qlat.shape
    npages = page_table.shape[1]
    ng = npages // pg
    nt = nrow // MLA_HEADS
    pad = PAGE_SIZE - cnew.shape[1]
    cnew = jnp.pad(cnew, ((0, 0), (0, pad), (0, 0)))
    rnew = jnp.pad(rnew, ((0, 0), (0, pad), (0, 0)))

    cache_kr_t = jnp.swapaxes(cache_kr, 2, 3)

    def page_spec(i, rows, width):
        return pl.BlockSpec((None, None, rows, width),
                            lambda b_, g, pt: (layer, pt[b_ * npages + g * pg + i], 0, 0))

    seq = lambda r, w: pl.BlockSpec((1, r, w), lambda b_, g, pt: (b_, 0, 0))
    grid_spec = pltpu.PrefetchScalarGridSpec(
        num_scalar_prefetch=1, grid=(b, ng),
        in_specs=[seq(nrow, MLA_KV_LORA), seq(nrow, MLA_ROPE), seq(PAGE_SIZE, MLA_KV_LORA), seq(PAGE_SIZE, MLA_ROPE)]
        + [page_spec(i, PAGE_SIZE, MLA_KV_LORA) for i in range(pg)]
        + [page_spec(i, MLA_ROPE, PAGE_SIZE) for i in range(pg)],
        out_specs=seq(nrow, MLA_KV_LORA),
        scratch_shapes=[pltpu.VMEM((pg * PAGE_SIZE, MLA_KV_LORA), BF16), pltpu.VMEM((MLA_ROPE, pg * PAGE_SIZE), BF16),
                        pltpu.VMEM((nrow, 1), F32), pltpu.VMEM((nrow, 1), F32), pltpu.VMEM((nrow, MLA_KV_LORA), F32)])
    return pl.pallas_call(
        functools.partial(_mla_sample_kernel, pg=pg, nt=nt, ng=ng),
        grid_spec=grid_spec,
        out_shape=jax.ShapeDtypeStruct((b, nrow, MLA_KV_LORA), F32),
        compiler_params=_params(("parallel", "arbitrary")), name="mla_sample",
    )(page_table.reshape(-1), qlat, qr, cnew, rnew, *([cache_ckv] * pg), *([cache_kr_t] * pg))


NEW_ROWS = 8


def _sb_sample_kernel(pt_ref, q_ref, knew_ref, vnew_ref, ck_hbm, cv_hbm, o_ref, kbuf, vbuf, knpad, vnpad, sem,
                      *, layer, npages, nt):
    b = pl.program_id(0)
    q = q_ref[0]
    nrow = q.shape[0]
    tri = _suffix_tri(PAGE_SIZE)

    def copies(p, slot):
        page = pt_ref[b * npages + p]
        return (pltpu.make_async_copy(ck_hbm.at[layer, page], kbuf.at[slot], sem.at[0, slot]),
                pltpu.make_async_copy(cv_hbm.at[layer, page], vbuf.at[slot], sem.at[1, slot]))

    def start(p, slot):
        for c in copies(p, slot):
            c.start()

    def wait(p, slot):
        for c in copies(p, slot):
            c.wait()

    slot_of = lambda p: lax.rem(npages - 1 - p, 2)
    start(npages - 1, 0)

    zpad = jnp.zeros((PAGE_SIZE - NEW_ROWS, SB_WIDTH), F32)
    knpad[...] = jnp.concatenate([knew_ref[0], zpad], axis=0)
    vnpad[...] = jnp.concatenate([vnew_ref[0], zpad], axis=0)
    tq = lax.broadcasted_iota(jnp.int32, (nrow, PAGE_SIZE), 0) // SB_HEADS
    sk = lax.broadcasted_iota(jnp.int32, (nrow, PAGE_SIZE), 1)
    acc, c = _sb_block(q, knpad[...].astype(BF16), vnpad[...].astype(BF16), tri, jnp.zeros((nrow, 1), F32),
                       jnp.logical_and(sk < tq, sk < nt))

    def cond(st):
        return jnp.logical_and(st[0] >= 0, st[3] > SB_EXIT)

    def body(st):
        p, c, acc, _ = st
        slot = slot_of(p)
        wait(p, slot)

        @pl.when(p > 0)
        def _():
            start(p - 1, 1 - slot)

        pv, c = _sb_block(q, kbuf[slot].astype(BF16), vbuf[slot].astype(BF16), tri, c, None, key_minor=True)
        return p - 1, c, acc + pv, jnp.max(c)

    p, _, acc, _ = lax.while_loop(cond, body, (jnp.int32(npages - 1), c, acc, jnp.max(c)))

    @pl.when(p >= 0)
    def _():
        wait(p, slot_of(p))

    own = (lax.broadcasted_iota(jnp.int32, (nrow, SB_WIDTH), 1) // SB_DIM
           == lax.broadcasted_iota(jnp.int32, (nrow, SB_WIDTH), 0) % SB_HEADS)
    o_ref[0] = jnp.where(own, acc, 0.0)


def _sb_sample(page_table, qm, knew, vnew, cache_k, cache_v, layer):
    b, nrow, _ = qm.shape
    npages = page_table.shape[1]
    nt = nrow // SB_HEADS
    pad = NEW_ROWS - knew.shape[1]
    knew = jnp.pad(knew, ((0, 0), (0, pad), (0, 0)))
    vnew = jnp.pad(vnew, ((0, 0), (0, pad), (0, 0)))
    key_minor = lambda c: jnp.transpose(c, (0, 1, 3, 4, 2)).reshape(c.shape[0], c.shape[1], SB_WIDTH, PAGE_SIZE)
    seq = lambda r: pl.BlockSpec((1, r, SB_WIDTH), lambda b_, pt: (b_, 0, 0))
    hbm = pl.BlockSpec(memory_space=pl.ANY)
    page_buf = pltpu.VMEM((2, SB_WIDTH, PAGE_SIZE), F32)
    new_buf = pltpu.VMEM((PAGE_SIZE, SB_WIDTH), F32)
    grid_spec = pltpu.PrefetchScalarGridSpec(
        num_scalar_prefetch=1, grid=(b,),
        in_specs=[seq(nrow), seq(NEW_ROWS), seq(NEW_ROWS), hbm, hbm],
        out_specs=seq(nrow),
        scratch_shapes=[page_buf, page_buf, new_buf, new_buf, pltpu.SemaphoreType.DMA((2, 2))])
    return pl.pallas_call(
        functools.partial(_sb_sample_kernel, layer=layer, npages=npages, nt=nt),
        grid_spec=grid_spec,
        out_shape=jax.ShapeDtypeStruct((b, nrow, SB_WIDTH), F32),
        compiler_params=_params(("arbitrary",)), name="sb_sample",
    )(page_table.reshape(-1), qm, knew, vnew, key_minor(cache_k), key_minor(cache_v))


def _mm_kernel(x_ref, w_ref, o_ref):
    o_ref[...] = _dot(x_ref[...].astype(BF16), w_ref[...]).astype(o_ref.dtype)


def _mm(x, w, out_dtype):
    m, n = x.shape[0], w.shape[1]
    return pl.pallas_call(
        _mm_kernel, grid=(1,), in_specs=[_full(x.shape), _full(w.shape)], out_specs=_full((m, n)),
        out_shape=jax.ShapeDtypeStruct((m, n), out_dtype), compiler_params=_params(("arbitrary",)), name="mm",
    )(x, w)


def _post_kernel(x_ref, a_ref, b_ref, wa_ref, wb_ref, g_ref, wu_ref, wd_ref, fg_ref, o_ref,
                 x1_scr, h_scr, acc_scr, *, final):
    j = pl.program_id(1)

    @pl.when(j == 0)
    def _():
        x1 = (x_ref[...] + _dot(a_ref[...].astype(BF16), wa_ref[...])
              + _dot(b_ref[...].astype(BF16), wb_ref[...]))
        x1_scr[...] = x1
        h_scr[...] = _rms(x1, g_ref[...]).astype(BF16)
        acc_scr[...] = jnp.zeros(acc_scr.shape, F32)

    u = jnp.maximum(_dot(h_scr[...], wu_ref[...]), 0.0)
    acc_scr[...] += _dot((u * u).astype(BF16), wd_ref[...])

    @pl.when(j == pl.num_programs(1) - 1)
    def _():
        y = x1_scr[...] + acc_scr[...]
        o_ref[...] = _rms(y, fg_ref[...]) if final else y


def _post(x, a, b, wa, wb, gain, wu, wd, fgain, final, tm, tf):
    m = x.shape[0]
    ff = wu.shape[1]
    row = lambda w: pl.BlockSpec((tm, w), lambda i, j: (i, 0))
    const = lambda shape: pl.BlockSpec(shape, lambda i, j: (0, 0))
    return pl.pallas_call(
        functools.partial(_post_kernel, final=final),
        grid=(m // tm, ff // tf),
        in_specs=[row(D_MODEL), row(a.shape[1]), row(b.shape[1]), const(wa.shape), const(wb.shape),
                  const((1, D_MODEL)), pl.BlockSpec((D_MODEL, tf), lambda i, j: (0, j)),
                  pl.BlockSpec((tf, D_MODEL), lambda i, j: (j, 0)), const((1, D_MODEL))],
        out_specs=row(D_MODEL),
        out_shape=jax.ShapeDtypeStruct((m, D_MODEL), F32),
        scratch_shapes=[pltpu.VMEM((tm, D_MODEL), F32), pltpu.VMEM((tm, D_MODEL), BF16),
                        pltpu.VMEM((tm, D_MODEL), F32)],
        compiler_params=_params(("parallel", "arbitrary")), name="post_mlp",
    )(x, a, b, wa, wb, gain.reshape(1, -1), wu, wd, fgain.reshape(1, -1))


HGW = HG_HEADS * HG_DK
OD_Q, OD_F, OD_I, OD_G = 0, HGW, 2 * HGW, 3 * HGW
OD_DQ = 4 * HGW
OD_DK = OD_DQ + SW_HEADS * SW_DIM
OD_DV = OD_DK + SW_KV_HEADS * SW_DIM
OD_N = OD_DV + SW_KV_HEADS * SW_DIM


def _odd_proj_kernel(x_ref, g_ref, win_ref, lbl_ref, q_out, k_out, v_out, lf_out, z_out, dq_out, dk_out, dv_out,
                     *, layer):
    h = _rms(x_ref[...], g_ref[...]).astype(BF16)
    hh = _dot(h, win_ref[...])
    lbl = lbl_ref[...]
    ex = jnp.exp(lbl - jnp.max(lbl, axis=0, keepdims=True))
    pr = ex / jnp.sum(ex, axis=0, keepdims=True)
    lb = jnp.zeros((1, HGW), F32)
    for i in range(1, layer + 1):
        lb = lb + pr[i:i + 1, :]
    hq = hh[:, OD_Q:OD_F]
    f = lb + (1.0 - lb) * jax.nn.sigmoid(hh[:, OD_F:OD_I])
    fc = jnp.maximum(f, FORGET_FLOOR)
    q_out[...] = hq * jax.nn.sigmoid(hq)
    k_out[...] = 1.0 - fc
    v_out[...] = hh[:, OD_I:OD_G]
    lf_out[...] = jnp.log(fc)
    hz = hh[:, OD_G:OD_DQ]
    z_out[...] = hz * jax.nn.sigmoid(hz)
    dq_out[...] = (hh[:, OD_DQ:OD_DK] * SW_SCALE).astype(BF16)
    dk_out[...] = hh[:, OD_DK:OD_DV]
    dv_out[...] = hh[:, OD_DV:OD_N]


def _odd_proj(x, gain, win, lb_logits, layer, tm):
    m = x.shape[0]
    row = lambda w: pl.BlockSpec((tm, w), lambda i: (i, 0))
    sds = jax.ShapeDtypeStruct
    kvw = SW_KV_HEADS * SW_DIM
    return pl.pallas_call(
        functools.partial(_odd_proj_kernel, layer=layer), grid=(m // tm,),
        in_specs=[row(D_MODEL), _full((1, D_MODEL)), _full(win.shape), _full(lb_logits.shape)],
        out_specs=[row(HGW)] * 5 + [row(SW_HEADS * SW_DIM), row(kvw), row(kvw)],
        out_shape=[sds((m, HGW), F32)] * 5 + [sds((m, SW_HEADS * SW_DIM), BF16), sds((m, kvw), F32), sds((m, kvw), F32)],
        compiler_params=_params(("parallel",)), name="odd_proj",
    )(x, gain.reshape(1, -1), win, lb_logits)


def _hgrn_kernel(q_ref, k_ref, v_ref, g_ref, z_ref, s0_ref, gn_ref, y_ref, sout_ref, gc_scr, st_scr, *, c, unroll):
    ci = pl.program_id(1)
    nsub = c // HG_SUB
    npair = HG_HEADS // 2

    @pl.when(ci == 0)
    def _():
        st_scr[...] = s0_ref[0]

    r = lax.broadcasted_iota(jnp.int32, (c, c), 0)
    cc = lax.broadcasted_iota(jnp.int32, (c, c), 1)
    lseg = jnp.logical_and(r >= cc, r // HG_SUB == cc // HG_SUB).astype(BF16)
    g = g_ref[0]
    g1 = g.astype(BF16)
    r1 = g - g1.astype(F32)
    g2 = r1.astype(BF16)
    g3 = (r1 - g2.astype(F32)).astype(BF16)
    gc_scr[...] = _dot(lseg, g1) + _dot(lseg, g2) + _dot(lseg, g3)

    same_head = (lax.broadcasted_iota(jnp.int32, (LANES, LANES), 0) // HG_DV
                 == lax.broadcasted_iota(jnp.int32, (LANES, LANES), 1) // HG_DK)
    ones_bd = same_head.astype(BF16)
    rows = lax.broadcasted_iota(jnp.int32, (HG_SUB, LANES), 0)
    gn = gn_ref[...]

    def sub_pair(off, pr):
        lanes = slice(pr * LANES, (pr + 1) * LANES)
        q = q_ref[0, pl.ds(off, HG_SUB), lanes]
        k = k_ref[0, pl.ds(off, HG_SUB), lanes]
        v = v_ref[0, pl.ds(off, HG_SUB), lanes]
        z = z_ref[0, pl.ds(off, HG_SUB), lanes]
        gc = gc_scr[pl.ds(off, HG_SUB), lanes]
        st = st_scr[pr]
        xs = []
        for s in range(HG_SUB):
            d = jnp.exp(jnp.minimum(gc - gc[s:s + 1, :], 0.0))
            xs.append(jnp.where(rows >= s, q * k[s:s + 1, :] * d, 0.0))
        x = jnp.concatenate(xs, axis=0).astype(BF16)
        a = _dot(x, ones_bd)
        o = jnp.zeros((HG_SUB, LANES), F32)
        for s in range(HG_SUB):
            o = o + a[s * HG_SUB:(s + 1) * HG_SUB, :] * v[s:s + 1, :]
        bd = jnp.where(same_head, jnp.concatenate([st, st], axis=0), 0.0).astype(BF16)
        o = o + _dot_nt((q * jnp.exp(gc)).astype(BF16), bd)
        hi, lo = _split2(o * o)
        ms = (_dot(hi, ones_bd) + _dot(lo, ones_bd)) * (1.0 / HG_DV)
        y_ref[0, pl.ds(off, HG_SUB), lanes] = o * lax.rsqrt(ms + NORM_EPS) * gn * z
        gl = gc[HG_SUB - 1:HG_SUB, :]
        kt = (k * jnp.exp(gl - gc)).astype(BF16)
        full = _dot(v.astype(BF16).T, kt)
        fm = jnp.where(same_head, full, 0.0)
        st_scr[pr] = jnp.exp(gl) * st + fm[:HG_DV, :] + fm[HG_DV:, :]

    def sub(i, carry):
        off = pl.multiple_of(i * HG_SUB, HG_SUB)
        for pr in range(npair):
            sub_pair(off, pr)
        return carry

    lax.fori_loop(0, nsub, sub, 0, unroll=unroll)

    @pl.when(ci == pl.num_programs(1) - 1)
    def _():
        sout_ref[0] = st_scr[...]


def _hgrn(q, k, v, g, z, s0, gnorm, c):
    b, t, _ = q.shape
    hp = HG_HEADS // 2
    tok = pl.BlockSpec((1, c, HGW), lambda b_, i: (b_, i, 0))
    stt = pl.BlockSpec((1, hp, HG_DV, LANES), lambda b_, i: (b_, 0, 0, 0))
    gn = jnp.tile(gnorm.reshape(1, -1), (1, LANES // HG_DV))
    return pl.pallas_call(
        functools.partial(_hgrn_kernel, c=c, unroll=min(4, c // HG_SUB)),
        grid=(b, t // c),
        in_specs=[tok] * 5 + [stt, pl.BlockSpec((1, LANES), lambda b_, i: (0, 0))],
        out_specs=[tok, stt],
        out_shape=[jax.ShapeDtypeStruct((b, t, HGW), F32), jax.ShapeDtypeStruct((b, hp, HG_DV, LANES), F32)],
        scratch_shapes=[pltpu.VMEM((c, HGW), F32), pltpu.VMEM((hp, HG_DV, LANES), F32)],
        compiler_params=_params(("parallel", "arbitrary")), name="hgrn2",
    )(q, k, v, g, z, s0, gn)


def _state_to_t(s):
    b = s.shape[0]
    return s.reshape(b, HG_HEADS // 2, 2, HG_DK, HG_DV).transpose(0, 1, 4, 2, 3).reshape(b, HG_HEADS // 2, HG_DV, 2 * HG_DK)


def _state_from_t(st):
    b = st.shape[0]
    return st.reshape(b, HG_HEADS // 2, HG_DV, 2, HG_DK).transpose(0, 1, 3, 4, 2).reshape(b, HG_HEADS, HG_DK, HG_DV)


def _rel_bucket_np(rel):
    n = np.maximum(rel, 0)
    exact = REL_BUCKETS // 2
    scaled = np.log(np.maximum(n, 1).astype(np.float32) / exact) / math.log(REL_MAX_DIST / exact)
    large = np.minimum(exact + (np.maximum(scaled, 0.0) * (REL_BUCKETS - exact)).astype(np.int32), REL_BUCKETS - 1)
    return np.where(n < exact, n, large).astype(np.int32)


def _swa_kernel(relb_ref, sink_ref, bucket_ref, q_ref, kp_ref, kc_ref, vp_ref, vc_ref, o_ref, bias_scr,
                *, first_has_no_prev):
    bi = pl.program_id(1)
    w = WINDOW

    @pl.when(jnp.logical_and(pl.program_id(0) == 0, bi == 0))
    def _():
        bucket = bucket_ref[...]
        for h in range(SW_HEADS):
            acc = jnp.zeros((w, 2 * w), F32)
            for bk in range(REL_BUCKETS):
                acc = jnp.where(bucket == bk, relb_ref[bk, h], acc)
            bias_scr[h] = acc

    qi = lax.broadcasted_iota(jnp.int32, (w, 2 * w), 0)
    kj = lax.broadcasted_iota(jnp.int32, (w, 2 * w), 1)
    rel = qi + w - kj
    allowed = jnp.logical_and(rel >= 0, rel <= WINDOW)
    if first_has_no_prev:
        allowed = jnp.logical_and(allowed, jnp.logical_or(bi > 0, kj >= w))
    lane_hi = lax.broadcasted_iota(jnp.int32, (1, LANES), 1) >= SW_DIM
    k = jnp.concatenate([kp_ref[0], kc_ref[0]], axis=0)
    v = jnp.concatenate([vp_ref[0], vc_ref[0]], axis=0)
    ksw = pltpu.roll(k, SW_DIM, 1)
    vsw = pltpu.roll(v, SW_DIM, 1)
    kg = (jnp.where(lane_hi, ksw, k).astype(BF16), jnp.where(lane_hi, k, ksw).astype(BF16))
    vg = (jnp.where(lane_hi, vsw, v).astype(BF16), jnp.where(lane_hi, v, vsw).astype(BF16))
    q = q_ref[0]
    outs = []
    for pair in range(SW_HEADS // 2):
        qb = q[:, pair * LANES:(pair + 1) * LANES]
        zq = jnp.zeros_like(qb)
        res = []
        for half in range(2):
            h = 2 * pair + half
            g = h // SW_GROUP
            qm = jnp.where(lane_hi, qb, zq) if half else jnp.where(lane_hi, zq, qb)
            s = jnp.where(allowed, _dot_nt(qm, kg[g]) + bias_scr[h], MASK_VALUE)
            sink = sink_ref[h]
            m = jnp.maximum(jnp.max(s, axis=-1, keepdims=True), sink)
            e = jnp.exp(s - m)
            den = jnp.sum(e, axis=-1, keepdims=True) + jnp.exp(sink - m)
            res.append(_dot((e / den).astype(BF16), vg[g]))
        outs.append(jnp.where(lane_hi, res[1], res[0]))
    o_ref[0] = jnp.concatenate(outs, axis=1)


def _swa(q, kprev_src, kcur_src, vprev_src, vcur_src, rel_bias, sinks, prev_is_shifted):
    b, s, _ = q.shape
    w = WINDOW
    nb = s // w
    kvw = SW_KV_HEADS * SW_DIM
    rel = (np.arange(w)[:, None] + w) - np.arange(2 * w)[None, :]
    bucket = jnp.asarray(_rel_bucket_np(rel))
    cur = pl.BlockSpec((1, w, kvw), lambda b_, i: (b_, i, 0))
    prev = pl.BlockSpec((1, w, kvw), lambda b_, i: (b_, jnp.maximum(i - 1, 0), 0)) if prev_is_shifted else cur
    smem = pl.BlockSpec(memory_space=pltpu.SMEM)
    return pl.pallas_call(
        functools.partial(_swa_kernel, first_has_no_prev=prev_is_shifted),
        grid=(b, nb),
        in_specs=[smem, smem, pl.BlockSpec((w, 2 * w), lambda b_, i: (0, 0)),
                  pl.BlockSpec((1, w, SW_HEADS * SW_DIM), lambda b_, i: (b_, i, 0)), prev, cur, prev, cur],
        out_specs=pl.BlockSpec((1, w, SW_HEADS * SW_DIM), lambda b_, i: (b_, i, 0)),
        out_shape=jax.ShapeDtypeStruct((b, s, SW_HEADS * SW_DIM), F32),
        scratch_shapes=[pltpu.VMEM((SW_HEADS, w, 2 * w), F32)],
        compiler_params=_params(("arbitrary", "arbitrary")), name="swa",
    )(rel_bias, sinks, bucket, q, kprev_src, kcur_src, vprev_src, vcur_src)


def _tile(m, pref):
    return pref if m % pref == 0 else m


def kernel(x_prompt, x_sample, cache_mla_ckv, cache_mla_krope, cache_sb_k, cache_sb_v, state_hgrn, state_swa_k, state_swa_v, page_table, mix_norm, w_in_even, mla_q_norm, mla_w_qb, mla_kv_norm, mla_w_kvb, w_out_even, w_in_odd, hgrn_lb_logits, hgrn_out_norm, swa_sinks, w_out_odd, rel_bias, mlp_norm, w_up, w_down, final_norm):
    bp, sp, d = x_prompt.shape
    bs, ts, _ = x_sample.shape
    depth = mix_norm.shape[0]
    past = page_table.shape[1] * PAGE_SIZE
    mp, ms = bp * sp, bs * ts
    xp = x_prompt.reshape(mp, d)
    xs = x_sample.reshape(ms, d)
    tab_p = jnp.tile(_rope_tables(jnp.arange(sp)), (bp, 1))
    tab_s = jnp.tile(_rope_tables(past + jnp.arange(ts)), (bs, 1))
    tm_p, tm_s = _tile(mp, 256), _tile(ms, 256)
    t_att = _tile(sp, 256)
    npages = page_table.shape[1]
    pg = next(c for c in (32, 8, 1) if npages % c == 0)
    nbuf = state_swa_k.shape[2]
    hg_pad = HG_SUB - ts
    kvw = SW_KV_HEADS * SW_DIM
    own_head = (jnp.arange(SB_WIDTH)[None, :] // SB_DIM == jnp.arange(SB_HEADS)[:, None])

    outs_p = {k: [] for k in ("ckv", "kr", "sbk", "sbv", "hg", "swk", "swv")}
    outs_s = {k: [] for k in ("ckv", "kr", "sbk", "sbv", "hg", "swk", "swv")}
    for layer in range(depth):
        last = layer == depth - 1
        if layer % 2 == 0:
            e = layer // 2
            wts = _even_weights(w_in_even[e], mla_w_qb[e], mla_w_kvb[e])
            wvbd = wts[6]
            q_p, k_p, v_p, c_p, r_p, sq_p, sk_p, sv_p = _even_proj(
                xp, mix_norm[layer], wts, mla_q_norm[e], mla_kv_norm[e], tab_p, False, tm_p)
            q_s, ql_s, c_s, r_s, sq_s, sk_s, sv_s = _even_proj(
                xs, mix_norm[layer], wts, mla_q_norm[e], mla_kv_norm[e], tab_s, True, tm_s)
            a_p = _mla_prompt(q_p.reshape(bp, sp, -1), k_p.reshape(bp, sp, -1), v_p.reshape(bp, sp, -1),
                              _tile(sp, 512))
            b_p = _sb_prompt(sq_p.reshape(bp, sp, -1), sk_p.reshape(bp, sp, -1), sv_p.reshape(bp, sp, -1), t_att)
            nrow = ts * MLA_HEADS
            qlat = ql_s.reshape(bs, nrow, MLA_KV_LORA)
            qrope = q_s.reshape(bs, nrow, HEAD_PAD)[:, :, MLA_NOPE:MLA_NOPE + MLA_ROPE]
            olat = _mla_sample(page_table, qlat, qrope, c_s.reshape(bs, ts, -1), r_s.reshape(bs, ts, -1),
                               cache_mla_ckv, cache_mla_krope, e, pg)
            a_s = _mm(olat.reshape(ms, MLA_HEADS * MLA_KV_LORA), wvbd, F32)
            qm = jnp.where(own_head[None, None], sq_s.reshape(bs, ts, 1, SB_WIDTH), jnp.zeros((), BF16))
            o_sb = _sb_sample(page_table, qm.reshape(bs, nrow, SB_WIDTH), sk_s.reshape(bs, ts, -1),
                              sv_s.reshape(bs, ts, -1), cache_sb_k, cache_sb_v, e)
            b_s = o_sb.reshape(bs, ts, SB_HEADS, SB_WIDTH).sum(axis=2).reshape(ms, SB_WIDTH)
            a_p, b_p = a_p.reshape(mp, -1), b_p.reshape(mp, -1)
            w_out = w_out_even[e].astype(BF16)
            wa, wb = w_out[:MLA_HEADS * MLA_V], w_out[MLA_HEADS * MLA_V:]
            wa_p = jnp.pad(wa.reshape(MLA_HEADS, MLA_V, d), ((0, 0), (0, HEAD_PAD - MLA_V), (0, 0))).reshape(-1, d)
            outs_p["ckv"].append(c_p.reshape(bp, sp, -1)); outs_p["kr"].append(r_p.reshape(bp, sp, -1))
            outs_p["sbk"].append(sk_p.reshape(bp, sp, SB_HEADS, SB_DIM)); outs_p["sbv"].append(sv_p.reshape(bp, sp, SB_HEADS, SB_DIM))
            outs_s["ckv"].append(c_s.reshape(bs, ts, -1)); outs_s["kr"].append(r_s.reshape(bs, ts, -1))
            outs_s["sbk"].append(sk_s.reshape(bs, ts, SB_HEADS, SB_DIM)); outs_s["sbv"].append(sv_s.reshape(bs, ts, SB_HEADS, SB_DIM))
        else:
            o = layer // 2
            win = w_in_odd[o].astype(BF16)
            lbl = hgrn_lb_logits.astype(F32)
            hq_p, hk_p, hv_p, hg_p, hz_p, dq_p, dk_p, dv_p = _odd_proj(xp, mix_norm[layer], win, lbl, o, tm_p)
            hq_s, hk_s, hv_s, hg_s, hz_s, dq_s, dk_s, dv_s = _odd_proj(xs, mix_norm[layer], win, lbl, o, tm_s)
            r3 = lambda a, b_: a.reshape(b_, -1, a.shape[-1])
            s0_p = jnp.zeros((bp, HG_HEADS // 2, HG_DV, LANES), F32)
            a_p, st_p = _hgrn(r3(hq_p, bp), r3(hk_p, bp), r3(hv_p, bp), r3(hg_p, bp), r3(hz_p, bp), s0_p,
                              hgrn_out_norm[o], _tile(sp, 256))
            padt = lambda a: jnp.pad(r3(a, bs), ((0, 0), (0, hg_pad), (0, 0)))
            a_s, st_s = _hgrn(padt(hq_s), padt(hk_s), padt(hv_s), padt(hg_s), padt(hz_s),
                              _state_to_t(state_hgrn[o].astype(F32)), hgrn_out_norm[o], HG_SUB)
            a_s = a_s[:, :ts].reshape(ms, -1)
            dk3, dv3 = r3(dk_p, bp), r3(dv_p, bp)
            b_p = _swa(r3(dq_p, bp), dk3, dk3, dv3, dv3, rel_bias, swa_sinks[o], True)
            padw = lambda a: jnp.pad(r3(a, bs), ((0, 0), (0, WINDOW - ts), (0, 0)))
            kbuf = state_swa_k[o].reshape(bs, nbuf, kvw)
            vbuf = state_swa_v[o].reshape(bs, nbuf, kvw)
            b_s = _swa(padw(dq_s), kbuf, padw(dk_s), vbuf, padw(dv_s), rel_bias, swa_sinks[o], False)
            b_s = b_s[:, :ts].reshape(ms, -1)
            a_p, b_p = a_p.reshape(mp, -1), b_p.reshape(mp, -1)
            w_out = w_out_odd[o].astype(BF16)
            wa, wb = w_out[:HG_HEADS * HG_DV], w_out[HG_HEADS * HG_DV:]
            wa_p = wa
            nkeep = min(WINDOW, sp)
            outs_p["hg"].append(_state_from_t(st_p)); outs_s["hg"].append(_state_from_t(st_s))
            outs_p["swk"].append(dk3[:, -nkeep:].reshape(bp, nkeep, SW_KV_HEADS, SW_DIM))
            outs_p["swv"].append(dv3[:, -nkeep:].reshape(bp, nkeep, SW_KV_HEADS, SW_DIM))
            kk = jnp.concatenate([kbuf, r3(dk_s, bs)], axis=1)[:, -nbuf:]
            vv = jnp.concatenate([vbuf, r3(dv_s, bs)], axis=1)[:, -nbuf:]
            outs_s["swk"].append(kk.reshape(bs, nbuf, SW_KV_HEADS, SW_DIM))
            outs_s["swv"].append(vv.reshape(bs, nbuf, SW_KV_HEADS, SW_DIM))
        wu, wd = w_up[layer].astype(BF16), w_down[layer].astype(BF16)
        xp = _post(xp, a_p, b_p, wa_p, wb, mlp_norm[layer], wu, wd, final_norm, last, _tile(mp, 512), _tile(D_FF, 1024))
        xs = _post(xs, a_s, b_s, wa, wb, mlp_norm[layer], wu, wd, final_norm, last, _tile(ms, 512), _tile(D_FF, 1024))
    st = lambda lst: jnp.stack(lst, axis=0)
    return (xp.reshape(bp, sp, d), xs.reshape(bs, ts, d),
            st(outs_p["ckv"]), st(outs_p["kr"]), st(outs_p["sbk"]), st(outs_p["sbv"]), st(outs_p["hg"]),
            st(outs_p["swk"]), st(outs_p["swv"]),
            st(outs_s["ckv"]), st(outs_s["kr"]), st(outs_s["sbk"]), st(outs_s["sbv"]), st(outs_s["hg"]),
            st(outs_s["swk"]), st(outs_s["swv"]))
```

```python
import functools
import math

import numpy as np
import jax
import jax.numpy as jnp
from jax import lax
from jax.experimental import pallas as pl
from jax.experimental.pallas import tpu as pltpu

F32 = jnp.float32
BF16 = jnp.bfloat16

D_MODEL = 1024
PAGE_SIZE = 128
MLA_HEADS = 8
MLA_NOPE = 64
MLA_ROPE = 32
MLA_V = 64
MLA_Q_LORA = 384
MLA_KV_LORA = 256
MLA_SCALE = (MLA_NOPE + MLA_ROPE) ** -0.5
MLA_QSCALE = MLA_SCALE * math.log2(math.e)
ROPE_THETA = 10000.0
SB_HEADS = 8
SB_DIM = 64
SB_WIDTH = SB_HEADS * SB_DIM
SB_SCALE = SB_DIM ** -0.5
HG_HEADS = 8
HG_DK = 64
HG_DV = 64
HG_SUB = 16
FORGET_FLOOR = 1e-30
SW_HEADS = 8
SW_KV_HEADS = 2
SW_GROUP = SW_HEADS // SW_KV_HEADS
SW_DIM = 64
SW_SCALE = SW_DIM ** -0.5
WINDOW = 128
REL_BUCKETS = 32
REL_MAX_DIST = 128
D_FF = 4 * D_MODEL
NORM_EPS = 1e-6
MASK_VALUE = -1e30
SB_EXIT = -150.0

LANES = 128
HEAD_PAD = 128
VMEM_LIMIT = 56 * 1024 * 1024

NT = (((1,), (1,)), ((), ()))


def _dot(a, b):
    return jnp.dot(a, b, preferred_element_type=F32)


def _dot_nt(a, b):
    return lax.dot_general(a, b, NT, preferred_element_type=F32)


def _rms(x, g):
    return x * lax.rsqrt(jnp.mean(x * x, axis=-1, keepdims=True) + NORM_EPS) * g


def _split2(x):
    hi = x.astype(BF16)
    lo = (x - hi.astype(F32)).astype(BF16)
    return hi, lo


def _params(sem):
    return pltpu.CompilerParams(dimension_semantics=sem, vmem_limit_bytes=VMEM_LIMIT)


def _full(shape):
    n = len(shape)
    return pl.BlockSpec(shape, lambda *_: (0,) * n)


EV_QA = 0
EV_KVA = MLA_Q_LORA
EV_SQ = EV_KVA + MLA_KV_LORA
EV_SK = EV_SQ + SB_WIDTH
EV_SV = EV_SK + SB_WIDTH
EV_KR = EV_SV + SB_WIDTH
EV_KRS = EV_KR + LANES
EV_N = EV_KRS + LANES


def _even_common(x_ref, g_ref, win_ref, qn_ref, wq_ref, wqs_ref, kvn_ref, tab_ref):
    h = _rms(x_ref[...], g_ref[...]).astype(BF16)
    hh = _dot(h, win_ref[...])
    tab = tab_ref[...]
    cos_q, sin_q = tab[:, 0:LANES], tab[:, LANES:2 * LANES]
    cos_k, sin_k = tab[:, 2 * LANES:3 * LANES], tab[:, 3 * LANES:4 * LANES]
    qa = _rms(hh[:, EV_QA:EV_KVA], qn_ref[...]).astype(BF16)
    q = (_dot(qa, wq_ref[...]) * jnp.tile(cos_q, (1, MLA_HEADS))
         + _dot(qa, wqs_ref[...]) * jnp.tile(sin_q, (1, MLA_HEADS)))
    ckv = _rms(hh[:, EV_KVA:EV_SQ], kvn_ref[...])
    kr = hh[:, EV_KR:EV_KRS] * cos_k + hh[:, EV_KRS:EV_N] * sin_k
    sq = (hh[:, EV_SQ:EV_SK] * SB_SCALE).astype(BF16)
    return q, ckv, kr, sq, hh[:, EV_SK:EV_SV], hh[:, EV_SV:EV_KR]


def _even_prompt_kernel(x_ref, g_ref, win_ref, qn_ref, wq_ref, wqs_ref, kvn_ref, tab_ref,
                        wk_ref, wv_ref, e_ref,
                        q_out, k_out, v_out, ckv_out, kr_out, sq_out, sk_out, sv_out):
    q, ckv, kr, sq, sk, sv = _even_common(x_ref, g_ref, win_ref, qn_ref, wq_ref, wqs_ref, kvn_ref, tab_ref)
    q_out[...] = q.astype(BF16)
    ckv_out[...] = ckv
    kr_out[...] = kr[:, :MLA_ROPE]
    sq_out[...] = sq
    sk_out[...] = sk
    sv_out[...] = sv
    cb = ckv.astype(BF16)
    k_out[...] = (_dot(cb, wk_ref[...]) + _dot(kr.astype(BF16), e_ref[...])).astype(BF16)
    lane = lax.broadcasted_iota(jnp.int32, (1, MLA_HEADS * HEAD_PAD), 1)
    ones_lane = jnp.where(lane % HEAD_PAD == MLA_V, 1.0, 0.0)
    v_out[...] = (_dot(cb, wv_ref[...]) + ones_lane).astype(BF16)


def _even_sample_kernel(x_ref, g_ref, win_ref, qn_ref, wq_ref, wqs_ref, kvn_ref, tab_ref,
                        wabs_ref,
                        q_out, qlat_out, ckv_out, kr_out, sq_out, sk_out, sv_out):
    q, ckv, kr, sq, sk, sv = _even_common(x_ref, g_ref, win_ref, qn_ref, wq_ref, wqs_ref, kvn_ref, tab_ref)
    qb = q.astype(BF16)
    q_out[...] = qb
    qlat_out[...] = _dot(qb, wabs_ref[...]).astype(BF16)
    ckv_out[...] = ckv
    kr_out[...] = kr[:, :MLA_ROPE]
    sq_out[...] = sq
    sk_out[...] = sk
    sv_out[...] = sv


def _rope_tables(pos):
    half = MLA_ROPE // 2
    inv = ROPE_THETA ** (-jnp.arange(half, dtype=F32) / half)
    ang = pos.astype(F32)[:, None] * inv[None, :]
    cos = jnp.cos(ang)
    sin = jnp.sin(ang)
    cos2 = jnp.concatenate([cos, cos], axis=-1)
    sin2 = jnp.concatenate([sin, sin], axis=-1)
    n = pos.shape[0]
    one = jnp.ones((n, MLA_NOPE), F32)
    z = lambda w: jnp.zeros((n, w), F32)
    cos_q = jnp.concatenate([one, cos2, z(HEAD_PAD - MLA_NOPE - MLA_ROPE)], -1) * MLA_QSCALE
    sin_q = jnp.concatenate([z(MLA_NOPE), sin2, z(HEAD_PAD - MLA_NOPE - MLA_ROPE)], -1) * MLA_QSCALE
    cos_k = jnp.concatenate([cos2, z(LANES - MLA_ROPE)], -1)
    sin_k = jnp.concatenate([sin2, z(LANES - MLA_ROPE)], -1)
    return jnp.concatenate([cos_q, sin_q, cos_k, sin_k], -1)


def _rot_cols(w):
    half = w.shape[-1] // 2
    return jnp.concatenate([-w[..., half:], w[..., :half]], axis=-1)


def _even_weights(w_in, w_qb, w_kvb):
    d = w_in.shape[0]
    o_kva, o_kr = MLA_Q_LORA, MLA_Q_LORA + MLA_KV_LORA
    o_sq = o_kr + MLA_ROPE
    w_kr = w_in[:, o_kr:o_sq]
    zpad = jnp.zeros((d, LANES - MLA_ROPE), F32)
    win = jnp.concatenate([w_in[:, :o_kr], w_in[:, o_sq:], w_kr, zpad, _rot_cols(w_kr), zpad], -1).astype(BF16)
    wq3 = w_qb.reshape(MLA_Q_LORA, MLA_HEADS, MLA_NOPE + MLA_ROPE)
    nope, rp = wq3[..., :MLA_NOPE], wq3[..., MLA_NOPE:]
    zq = lambda w: jnp.zeros((MLA_Q_LORA, MLA_HEADS, w), F32)
    pad = HEAD_PAD - MLA_NOPE - MLA_ROPE
    wq = jnp.concatenate([nope, rp, zq(pad)], -1).reshape(MLA_Q_LORA, -1).astype(BF16)
    wqs = jnp.concatenate([zq(MLA_NOPE), _rot_cols(rp), zq(pad)], -1).reshape(MLA_Q_LORA, -1).astype(BF16)
    wkv3 = w_kvb.reshape(MLA_KV_LORA, MLA_HEADS, MLA_NOPE + MLA_V)
    knope, wv3 = wkv3[..., :MLA_NOPE], wkv3[..., MLA_NOPE:]
    wk = jnp.concatenate([knope, jnp.zeros((MLA_KV_LORA, MLA_HEADS, HEAD_PAD - MLA_NOPE), F32)], -1)
    wk = wk.reshape(MLA_KV_LORA, -1).astype(BF16)
    wv = jnp.concatenate([wv3, jnp.zeros((MLA_KV_LORA, MLA_HEADS, HEAD_PAD - MLA_V), F32)], -1)
    wv = wv.reshape(MLA_KV_LORA, -1).astype(BF16)
    eye = jnp.eye(MLA_HEADS, dtype=F32)
    kt = jnp.concatenate([knope, jnp.zeros((MLA_KV_LORA, MLA_HEADS, HEAD_PAD - MLA_NOPE), F32)], -1)
    wabs = jnp.einsum('chn,hg->hngc', kt, eye).reshape(MLA_HEADS * HEAD_PAD, MLA_HEADS * MLA_KV_LORA).astype(BF16)
    wvbd = jnp.einsum('chv,hg->hcgv', wv3, eye).reshape(MLA_HEADS * MLA_KV_LORA, MLA_HEADS * MLA_V).astype(BF16)
    return win, wq, wqs, wk, wv, wabs, wvbd


def _krope_place():
    e = np.zeros((LANES, MLA_HEADS * HEAD_PAD), np.float32)
    for h in range(MLA_HEADS):
        for i in range(MLA_ROPE):
            e[i, h * HEAD_PAD + MLA_NOPE + i] = 1.0
    return jnp.asarray(e, BF16)


def _even_proj(x, gain, weights, q_norm, kv_norm, tab, sample, tm):
    m = x.shape[0]
    win, wq, wqs, wk, wv, wabs, _ = weights
    hq = MLA_HEADS * HEAD_PAD
    row = lambda w: pl.BlockSpec((tm, w), lambda i: (i, 0))
    common_in = [row(D_MODEL), _full((1, D_MODEL)), _full(win.shape), _full((1, MLA_Q_LORA)), _full(wq.shape),
                 _full(wqs.shape), _full((1, MLA_KV_LORA)), row(4 * LANES)]
    common_args = [x, gain.reshape(1, -1), win, q_norm.reshape(1, -1), wq, wqs, kv_norm.reshape(1, -1), tab]
    sds = jax.ShapeDtypeStruct
    tail_shapes = [sds((m, MLA_KV_LORA), F32), sds((m, MLA_ROPE), F32), sds((m, SB_WIDTH), BF16),
                   sds((m, SB_WIDTH), F32), sds((m, SB_WIDTH), F32)]
    tail_specs = [row(MLA_KV_LORA), row(MLA_ROPE), row(SB_WIDTH), row(SB_WIDTH), row(SB_WIDTH)]
    if sample:
        return pl.pallas_call(
            _even_sample_kernel, grid=(m // tm,),
            in_specs=common_in + [_full(wabs.shape)],
            out_specs=[row(hq), row(MLA_HEADS * MLA_KV_LORA)] + tail_specs,
            out_shape=[sds((m, hq), BF16), sds((m, MLA_HEADS * MLA_KV_LORA), BF16)] + tail_shapes,
            compiler_params=_params(("parallel",)), name="even_proj_sample",
        )(*common_args, wabs)
    e = _krope_place()
    return pl.pallas_call(
        _even_prompt_kernel, grid=(m // tm,),
        in_specs=common_in + [_full(wk.shape), _full(wv.shape), _full(e.shape)],
        out_specs=[row(hq), row(hq), row(hq)] + tail_specs,
        out_shape=[sds((m, hq), BF16), sds((m, hq), BF16), sds((m, hq), BF16)] + tail_shapes,
        compiler_params=_params(("parallel",)), name="even_proj_prompt",
    )(*common_args, wk, wv, e)


def _mla_prompt_kernel(q_ref, k_ref, v_ref, o_ref, *, t, rt):
    qi = pl.program_id(2)
    nsub = t // rt
    qs = [q_ref[0, r * rt:(r + 1) * rt, :] for r in range(nsub)]
    rows = lax.broadcasted_iota(jnp.int32, (rt, t), 0)
    cols = lax.broadcasted_iota(jnp.int32, (rt, t), 1)

    def block(jb, carry, diag):
        ks = pl.multiple_of(jb * t, t)
        k = k_ref[0, pl.ds(ks, t), :]
        v = v_ref[0, pl.ds(ks, t), :]
        ss = [_dot_nt(qs[r], k) for r in range(nsub)]
        if diag:
            ss = [jnp.where(cols <= rows + r * rt, ss[r], MASK_VALUE) for r in range(nsub)]
        mns = [jnp.maximum(carry[2 * r], jnp.max(ss[r], axis=-1, keepdims=True)) for r in range(nsub)]
        ps = [jnp.exp2(ss[r] - mns[r]).astype(BF16) for r in range(nsub)]
        out = []
        for r in range(nsub):
            out += [mns[r], jnp.exp2(carry[2 * r] - mns[r]) * carry[2 * r + 1] + _dot(ps[r], v)]
        return tuple(out)

    init = (jnp.full((rt, 1), MASK_VALUE, F32), jnp.zeros((rt, HEAD_PAD), F32)) * nsub
    carry = lax.fori_loop(0, qi, lambda jb, c: block(jb, c, False), init)
    carry = block(qi, carry, True)
    for r in range(nsub):
        acc = carry[2 * r + 1]
        o_ref[0, r * rt:(r + 1) * rt, :] = acc / acc[:, MLA_V:MLA_V + 1]


def _mla_prompt(q, k, v, t):
    b, s, _ = q.shape
    tile = pl.BlockSpec((1, t, HEAD_PAD), lambda b_, h, i: (b_, i, h))
    seq = pl.BlockSpec((1, s, HEAD_PAD), lambda b_, h, i: (b_, 0, h))
    return pl.pallas_call(
        functools.partial(_mla_prompt_kernel, t=t, rt=min(t, 128)),
        grid=(b, MLA_HEADS, s // t),
        in_specs=[tile, seq, seq],
        out_specs=tile,
        out_shape=jax.ShapeDtypeStruct((b, s, MLA_HEADS * HEAD_PAD), F32),
        compiler_params=_params(("parallel", "parallel", "arbitrary")), name="mla_prompt",
    )(q, k, v)


def _log_sig_neg(z):
    return -(jnp.maximum(z, 0.0) + jnp.log(1.0 + jnp.exp(-jnp.abs(z))))


def _suffix_tri(n):
    return (lax.broadcasted_iota(jnp.int32, (n, n), 0) >= lax.broadcasted_iota(jnp.int32, (n, n), 1)).astype(BF16)


def _sb_blocks(qhs, kb, vb, tri, carries, alloweds, key_minor=False):
    n = len(qhs)
    zs = [_dot(qh, kb) if key_minor else _dot_nt(qh, kb) for qh in qhs]
    lsms = [_log_sig_neg(z) for z in zs]
    lsms = [l if a is None else jnp.where(a, l, 0.0) for l, a in zip(lsms, alloweds)]
    parts = [_split2(l) for l in lsms]
    css = [_dot(hi, tri) + _dot(lo, tri) for hi, lo in parts]
    las = [zs[i] + css[i] + carries[i] for i in range(n)]
    las = [l if a is None else jnp.where(a, l, MASK_VALUE) for l, a in zip(las, alloweds)]
    ps = [jnp.exp(l).astype(BF16) for l in las]
    pvs = [_dot_nt(p, vb) if key_minor else _dot(p, vb) for p in ps]
    return pvs, [carries[i] + css[i][:, :1] for i in range(n)]


def _sb_block(qh, kb, vb, tri, carry, allowed, key_minor=False):
    pvs, cs = _sb_blocks([qh], kb, vb, tri, [carry], [allowed], key_minor)
    return pvs[0], cs[0]


def _sb_prompt_kernel(q_ref, k_ref, v_ref, o_ref, *, t, rt):
    qi = pl.program_id(2)
    nsub = t // rt
    head1 = lax.broadcasted_iota(jnp.int32, (1, LANES), 1) >= SB_DIM
    qhs = []
    for r in range(nsub):
        q = q_ref[0, r * rt:(r + 1) * rt, :]
        zq = jnp.zeros_like(q)
        qhs += [jnp.where(head1, zq, q), jnp.where(head1, q, zq)]
    tri = _suffix_tri(t)
    rows = lax.broadcasted_iota(jnp.int32, (rt, t), 0)
    cols = lax.broadcasted_iota(jnp.int32, (rt, t), 1)
    strict = [cols < rows + (i // 2) * rt for i in range(2 * nsub)]

    def block(jb, cs, accs, alloweds):
        ks = pl.multiple_of(jb * t, t)
        kb = k_ref[0, pl.ds(ks, t), :].astype(BF16)
        vb = v_ref[0, pl.ds(ks, t), :].astype(BF16)
        pvs, cs = _sb_blocks(qhs, kb, vb, tri, cs, alloweds)
        accs = [accs[r] + jnp.where(head1, pvs[2 * r + 1], pvs[2 * r]) for r in range(nsub)]
        return cs, accs

    cmax = lambda cs: jnp.max(functools.reduce(jnp.maximum, cs))
    cs, accs = block(qi, [jnp.zeros((rt, 1), F32)] * (2 * nsub), [jnp.zeros((rt, LANES), F32)] * nsub, strict)

    def cond(st):
        return jnp.logical_and(st[0] >= 0, st[3] > SB_EXIT)

    def body(st):
        jb, cs, accs, _ = st
        cs, accs = block(jb, list(cs), list(accs), [None] * (2 * nsub))
        return jb - 1, tuple(cs), tuple(accs), cmax(cs)

    st = lax.while_loop(cond, body, (qi - 1, tuple(cs), tuple(accs), cmax(cs)))
    for r in range(nsub):
        o_ref[0, r * rt:(r + 1) * rt, :] = st[2][r]


def _sb_prompt(q, k, v, t):
    b, s, _ = q.shape
    hp = SB_HEADS // 2
    spec_q = pl.BlockSpec((1, t, LANES), lambda b_, h, i: (b_, i, h))
    spec_kv = pl.BlockSpec((1, s, LANES), lambda b_, h, i: (b_, 0, h))
    return pl.pallas_call(
        functools.partial(_sb_prompt_kernel, t=t, rt=min(t, 128)),
        grid=(b, hp, s // t),
        in_specs=[spec_q, spec_kv, spec_kv],
        out_specs=spec_q,
        out_shape=jax.ShapeDtypeStruct((b, s, SB_WIDTH), F32),
        compiler_params=_params(("parallel", "parallel", "arbitrary")), name="sb_prompt",
    )(q, k, v)


def _mla_sample_kernel(pt_ref, qlat_ref, qr_ref, cnew_ref, rnew_ref, *rest, pg, nt, ng):
    ckv_refs = rest[:pg]
    kr_refs = rest[pg:2 * pg]
    o_ref, kbuf, rbuf, m_scr, l_scr, acc_scr = rest[2 * pg:]
    g = pl.program_id(1)
    ql = qlat_ref[0]
    qr = qr_ref[0]
    nrow = ql.shape[0]

    def update(s, vals):
        m = m_scr[...]
        mn = jnp.maximum(m, jnp.max(s, axis=-1, keepdims=True))
        a = jnp.exp2(m - mn)
        p = jnp.exp2(s - mn)
        l_scr[...] = a * l_scr[...] + jnp.sum(p, axis=-1, keepdims=True)
        acc_scr[...] = a * acc_scr[...] + _dot(p.astype(BF16), vals)
        m_scr[...] = mn

    @pl.when(g == 0)
    def _():
        m_scr[...] = jnp.full(m_scr.shape, MASK_VALUE, F32)
        l_scr[...] = jnp.zeros(l_scr.shape, F32)
        acc_scr[...] = jnp.zeros(acc_scr.shape, F32)
        cn = cnew_ref[0].astype(BF16)
        s = _dot_nt(ql, cn) + _dot_nt(qr, rnew_ref[0].astype(BF16))
        tq = lax.broadcasted_iota(jnp.int32, s.shape, 0) // MLA_HEADS
        sk = lax.broadcasted_iota(jnp.int32, s.shape, 1)
        s = jnp.where(jnp.logical_and(sk <= tq, sk < nt), s, MASK_VALUE)
        update(s, cn)

    for i in range(pg):
        kbuf[i * PAGE_SIZE:(i + 1) * PAGE_SIZE, :] = ckv_refs[i][...].astype(BF16)
        rbuf[:, i * PAGE_SIZE:(i + 1) * PAGE_SIZE] = kr_refs[i][...].astype(BF16)
    kb = kbuf[...]
    update(_dot_nt(ql, kb) + _dot(qr, rbuf[...]), kb)

    @pl.when(g == ng - 1)
    def _():
        o_ref[0] = acc_scr[...] / l_scr[...]


def _mla_sample(page_table, qlat, qr, cnew, rnew, cache_ckv, cache_kr, layer, pg):
    b, nrow, _ = qlat.shape
    npages = page_table.shape[1]
    ng = npages // pg
    nt = nrow // MLA_HEADS
    pad = PAGE_SIZE - cnew.shape[1]
    cnew = jnp.pad(cnew, ((0, 0), (0, pad), (0, 0)))
    rnew = jnp.pad(rnew, ((0, 0), (0, pad), (0, 0)))

    cache_kr_t = jnp.swapaxes(cache_kr, 2, 3)

    def page_spec(i, rows, width):
        return pl.BlockSpec((None, None, rows, width),
                            lambda b_, g, pt: (layer, pt[b_ * npages + g * pg + i], 0, 0))

    seq = lambda r, w: pl.BlockSpec((1, r, w), lambda b_, g, pt: (b_, 0, 0))
    grid_spec = pltpu.PrefetchScalarGridSpec(
        num_scalar_prefetch=1, grid=(b, ng),
        in_specs=[seq(nrow, MLA_KV_LORA), seq(nrow, MLA_ROPE), seq(PAGE_SIZE, MLA_KV_LORA), seq(PAGE_SIZE, MLA_ROPE)]
        + [page_spec(i, PAGE_SIZE, MLA_KV_LORA) for i in range(pg)]
        + [page_spec(i, MLA_ROPE, PAGE_SIZE) for i in range(pg)],
        out_specs=seq(nrow, MLA_KV_LORA),
        scratch_shapes=[pltpu.VMEM((pg * PAGE_SIZE, MLA_KV_LORA), BF16), pltpu.VMEM((MLA_ROPE, pg * PAGE_SIZE), BF16),
                        pltpu.VMEM((nrow, 1), F32), pltpu.VMEM((nrow, 1), F32), pltpu.VMEM((nrow, MLA_KV_LORA), F32)])
    return pl.pallas_call(
        functools.partial(_mla_sample_kernel, pg=pg, nt=nt, ng=ng),
        grid_spec=grid_spec,
        out_shape=jax.ShapeDtypeStruct((b, nrow, MLA_KV_LORA), F32),
        compiler_params=_params(("parallel", "arbitrary")), name="mla_sample",
    )(page_table.reshape(-1), qlat, qr, cnew, rnew, *([cache_ckv] * pg), *([cache_kr_t] * pg))


NEW_ROWS = 8


def _sb_sample_kernel(pt_ref, q_ref, knew_ref, vnew_ref, ck_hbm, cv_hbm, o_ref, kbuf, vbuf, knpad, vnpad, sem,
                      *, layer, npages, nt):
    b = pl.program_id(0)
    q = q_ref[0]
    nrow = q.shape[0]
    tri = _suffix_tri(PAGE_SIZE)

    def copies(p, slot):
        page = pt_ref[b * npages + p]
        return (pltpu.make_async_copy(ck_hbm.at[layer, page], kbuf.at[slot], sem.at[0, slot]),
                pltpu.make_async_copy(cv_hbm.at[layer, page], vbuf.at[slot], sem.at[1, slot]))

    def start(p, slot):
        for c in copies(p, slot):
            c.start()

    def wait(p, slot):
        for c in copies(p, slot):
            c.wait()

    slot_of = lambda p: lax.rem(npages - 1 - p, 2)
    start(npages - 1, 0)

    zpad = jnp.zeros((PAGE_SIZE - NEW_ROWS, SB_WIDTH), F32)
    knpad[...] = jnp.concatenate([knew_ref[0], zpad], axis=0)
    vnpad[...] = jnp.concatenate([vnew_ref[0], zpad], axis=0)
    tq = lax.broadcasted_iota(jnp.int32, (nrow, PAGE_SIZE), 0) // SB_HEADS
    sk = lax.broadcasted_iota(jnp.int32, (nrow, PAGE_SIZE), 1)
    acc, c = _sb_block(q, knpad[...].astype(BF16), vnpad[...].astype(BF16), tri, jnp.zeros((nrow, 1), F32),
                       jnp.logical_and(sk < tq, sk < nt))

    def cond(st):
        return jnp.logical_and(st[0] >= 0, st[3] > SB_EXIT)

    def body(st):
        p, c, acc, _ = st
        slot = slot_of(p)
        wait(p, slot)

        @pl.when(p > 0)
        def _():
            start(p - 1, 1 - slot)

        pv, c = _sb_block(q, kbuf[slot].astype(BF16), vbuf[slot].astype(BF16), tri, c, None, key_minor=True)
        return p - 1, c, acc + pv, jnp.max(c)

    p, _, acc, _ = lax.while_loop(cond, body, (jnp.int32(npages - 1), c, acc, jnp.max(c)))

    @pl.when(p >= 0)
    def _():
        wait(p, slot_of(p))

    own = (lax.broadcasted_iota(jnp.int32, (nrow, SB_WIDTH), 1) // SB_DIM
           == lax.broadcasted_iota(jnp.int32, (nrow, SB_WIDTH), 0) % SB_HEADS)
    o_ref[0] = jnp.where(own, acc, 0.0)


def _sb_sample(page_table, qm, knew, vnew, cache_k, cache_v, layer):
    b, nrow, _ = qm.shape
    npages = page_table.shape[1]
    nt = nrow // SB_HEADS
    pad = NEW_ROWS - knew.shape[1]
    knew = jnp.pad(knew, ((0, 0), (0, pad), (0, 0)))
    vnew = jnp.pad(vnew, ((0, 0), (0, pad), (0, 0)))
    key_minor = lambda c: jnp.transpose(c, (0, 1, 3, 4, 2)).reshape(c.shape[0], c.shape[1], SB_WIDTH, PAGE_SIZE)
    seq = lambda r: pl.BlockSpec((1, r, SB_WIDTH), lambda b_, pt: (b_, 0, 0))
    hbm = pl.BlockSpec(memory_space=pl.ANY)
    page_buf = pltpu.VMEM((2, SB_WIDTH, PAGE_SIZE), F32)
    new_buf = pltpu.VMEM((PAGE_SIZE, SB_WIDTH), F32)
    grid_spec = pltpu.PrefetchScalarGridSpec(
        num_scalar_prefetch=1, grid=(b,),
        in_specs=[seq(nrow), seq(NEW_ROWS), seq(NEW_ROWS), hbm, hbm],
        out_specs=seq(nrow),
        scratch_shapes=[page_buf, page_buf, new_buf, new_buf, pltpu.SemaphoreType.DMA((2, 2))])
    return pl.pallas_call(
        functools.partial(_sb_sample_kernel, layer=layer, npages=npages, nt=nt),
        grid_spec=grid_spec,
        out_shape=jax.ShapeDtypeStruct((b, nrow, SB_WIDTH), F32),
        compiler_params=_params(("arbitrary",)), name="sb_sample",
    )(page_table.reshape(-1), qm, knew, vnew, key_minor(cache_k), key_minor(cache_v))


def _mm_kernel(x_ref, w_ref, o_ref):
    o_ref[...] = _dot(x_ref[...].astype(BF16), w_ref[...]).astype(o_ref.dtype)


def _mm(x, w, out_dtype):
    m, n = x.shape[0], w.shape[1]
    return pl.pallas_call(
        _mm_kernel, grid=(1,), in_specs=[_full(x.shape), _full(w.shape)], out_specs=_full((m, n)),
        out_shape=jax.ShapeDtypeStruct((m, n), out_dtype), compiler_params=_params(("arbitrary",)), name="mm",
    )(x, w)


def _post_kernel(x_ref, a_ref, b_ref, wa_ref, wb_ref, g_ref, wu_ref, wd_ref, fg_ref, o_ref,
                 x1_scr, h_scr, acc_scr, *, final):
    j = pl.program_id(1)

    @pl.when(j == 0)
    def _():
        x1 = (x_ref[...] + _dot(a_ref[...].astype(BF16), wa_ref[...])
              + _dot(b_ref[...].astype(BF16), wb_ref[...]))
        x1_scr[...] = x1
        h_scr[...] = _rms(x1, g_ref[...]).astype(BF16)
        acc_scr[...] = jnp.zeros(acc_scr.shape, F32)

    u = jnp.maximum(_dot(h_scr[...], wu_ref[...]), 0.0)
    acc_scr[...] += _dot((u * u).astype(BF16), wd_ref[...])

    @pl.when(j == pl.num_programs(1) - 1)
    def _():
        y = x1_scr[...] + acc_scr[...]
        o_ref[...] = _rms(y, fg_ref[...]) if final else y


def _post(x, a, b, wa, wb, gain, wu, wd, fgain, final, tm, tf):
    m = x.shape[0]
    ff = wu.shape[1]
    row = lambda w: pl.BlockSpec((tm, w), lambda i, j: (i, 0))
    const = lambda shape: pl.BlockSpec(shape, lambda i, j: (0, 0))
    return pl.pallas_call(
        functools.partial(_post_kernel, final=final),
        grid=(m // tm, ff // tf),
        in_specs=[row(D_MODEL), row(a.shape[1]), row(b.shape[1]), const(wa.shape), const(wb.shape),
                  const((1, D_MODEL)), pl.BlockSpec((D_MODEL, tf), lambda i, j: (0, j)),
                  pl.BlockSpec((tf, D_MODEL), lambda i, j: (j, 0)), const((1, D_MODEL))],
        out_specs=row(D_MODEL),
        out_shape=jax.ShapeDtypeStruct((m, D_MODEL), F32),
        scratch_shapes=[pltpu.VMEM((tm, D_MODEL), F32), pltpu.VMEM((tm, D_MODEL), BF16),
                        pltpu.VMEM((tm, D_MODEL), F32)],
        compiler_params=_params(("parallel", "arbitrary")), name="post_mlp",
    )(x, a, b, wa, wb, gain.reshape(1, -1), wu, wd, fgain.reshape(1, -1))


HGW = HG_HEADS * HG_DK
OD_Q, OD_F, OD_I, OD_G = 0, HGW, 2 * HGW, 3 * HGW
OD_DQ = 4 * HGW
OD_DK = OD_DQ + SW_HEADS * SW_DIM
OD_DV = OD_DK + SW_KV_HEADS * SW_DIM
OD_N = OD_DV + SW_KV_HEADS * SW_DIM


def _odd_proj_kernel(x_ref, g_ref, win_ref, lbl_ref, q_out, k_out, v_out, lf_out, z_out, dq_out, dk_out, dv_out,
                     *, layer):
    h = _rms(x_ref[...], g_ref[...]).astype(BF16)
    hh = _dot(h, win_ref[...])
    lbl = lbl_ref[...]
    ex = jnp.exp(lbl - jnp.max(lbl, axis=0, keepdims=True))
    pr = ex / jnp.sum(ex, axis=0, keepdims=True)
    lb = jnp.zeros((1, HGW), F32)
    for i in range(1, layer + 1):
        lb = lb + pr[i:i + 1, :]
    hq = hh[:, OD_Q:OD_F]
    f = lb + (1.0 - lb) * jax.nn.sigmoid(hh[:, OD_F:OD_I])
    fc = jnp.maximum(f, FORGET_FLOOR)
    q_out[...] = hq * jax.nn.sigmoid(hq)
    k_out[...] = 1.0 - fc
    v_out[...] = hh[:, OD_I:OD_G]
    lf_out[...] = jnp.log(fc)
    hz = hh[:, OD_G:OD_DQ]
    z_out[...] = hz * jax.nn.sigmoid(hz)
    dq_out[...] = (hh[:, OD_DQ:OD_DK] * SW_SCALE).astype(BF16)
    dk_out[...] = hh[:, OD_DK:OD_DV]
    dv_out[...] = hh[:, OD_DV:OD_N]


def _odd_proj(x, gain, win, lb_logits, layer, tm):
    m = x.shape[0]
    row = lambda w: pl.BlockSpec((tm, w), lambda i: (i, 0))
    sds = jax.ShapeDtypeStruct
    kvw = SW_KV_HEADS * SW_DIM
    return pl.pallas_call(
        functools.partial(_odd_proj_kernel, layer=layer), grid=(m // tm,),
        in_specs=[row(D_MODEL), _full((1, D_MODEL)), _full(win.shape), _full(lb_logits.shape)],
        out_specs=[row(HGW)] * 5 + [row(SW_HEADS * SW_DIM), row(kvw), row(kvw)],
        out_shape=[sds((m, HGW), F32)] * 5 + [sds((m, SW_HEADS * SW_DIM), BF16), sds((m, kvw), F32), sds((m, kvw), F32)],
        compiler_params=_params(("parallel",)), name="odd_proj",
    )(x, gain.reshape(1, -1), win, lb_logits)


def _hgrn_kernel(q_ref, k_ref, v_ref, g_ref, z_ref, s0_ref, gn_ref, y_ref, sout_ref, gc_scr, st_scr, *, c, unroll):
    ci = pl.program_id(1)
    nsub = c // HG_SUB
    npair = HG_HEADS // 2

    @pl.when(ci == 0)
    def _():
        st_scr[...] = s0_ref[0]

    r = lax.broadcasted_iota(jnp.int32, (c, c), 0)
    cc = lax.broadcasted_iota(jnp.int32, (c, c), 1)
    lseg = jnp.logical_and(r >= cc, r // HG_SUB == cc // HG_SUB).astype(BF16)
    g = g_ref[0]
    g1 = g.astype(BF16)
    r1 = g - g1.astype(F32)
    g2 = r1.astype(BF16)
    g3 = (r1 - g2.astype(F32)).astype(BF16)
    gc_scr[...] = _dot(lseg, g1) + _dot(lseg, g2) + _dot(lseg, g3)

    same_head = (lax.broadcasted_iota(jnp.int32, (LANES, LANES), 0) // HG_DV
                 == lax.broadcasted_iota(jnp.int32, (LANES, LANES), 1) // HG_DK)
    ones_bd = same_head.astype(BF16)
    rows = lax.broadcasted_iota(jnp.int32, (HG_SUB, LANES), 0)
    gn = gn_ref[...]

    def sub_pair(off, pr):
        lanes = slice(pr * LANES, (pr + 1) * LANES)
        q = q_ref[0, pl.ds(off, HG_SUB), lanes]
        k = k_ref[0, pl.ds(off, HG_SUB), lanes]
        v = v_ref[0, pl.ds(off, HG_SUB), lanes]
        z = z_ref[0, pl.ds(off, HG_SUB), lanes]
        gc = gc_scr[pl.ds(off, HG_SUB), lanes]
        st = st_scr[pr]
        xs = []
        for s in range(HG_SUB):
            d = jnp.exp(jnp.minimum(gc - gc[s:s + 1, :], 0.0))
            xs.append(jnp.where(rows >= s, q * k[s:s + 1, :] * d, 0.0))
        x = jnp.concatenate(xs, axis=0).astype(BF16)
        a = _dot(x, ones_bd)
        o = jnp.zeros((HG_SUB, LANES), F32)
        for s in range(HG_SUB):
            o = o + a[s * HG_SUB:(s + 1) * HG_SUB, :] * v[s:s + 1, :]
        bd = jnp.where(same_head, jnp.concatenate([st, st], axis=0), 0.0).astype(BF16)
        o = o + _dot_nt((q * jnp.exp(gc)).astype(BF16), bd)
        hi, lo = _split2(o * o)
        ms = (_dot(hi, ones_bd) + _dot(lo, ones_bd)) * (1.0 / HG_DV)
        y_ref[0, pl.ds(off, HG_SUB), lanes] = o * lax.rsqrt(ms + NORM_EPS) * gn * z
        gl = gc[HG_SUB - 1:HG_SUB, :]
        kt = (k * jnp.exp(gl - gc)).astype(BF16)
        full = _dot(v.astype(BF16).T, kt)
        fm = jnp.where(same_head, full, 0.0)
        st_scr[pr] = jnp.exp(gl) * st + fm[:HG_DV, :] + fm[HG_DV:, :]

    def sub(i, carry):
        off = pl.multiple_of(i * HG_SUB, HG_SUB)
        for pr in range(npair):
            sub_pair(off, pr)
        return carry

    lax.fori_loop(0, nsub, sub, 0, unroll=unroll)

    @pl.when(ci == pl.num_programs(1) - 1)
    def _():
        sout_ref[0] = st_scr[...]


def _hgrn(q, k, v, g, z, s0, gnorm, c):
    b, t, _ = q.shape
    hp = HG_HEADS // 2
    tok = pl.BlockSpec((1, c, HGW), lambda b_, i: (b_, i, 0))
    stt = pl.BlockSpec((1, hp, HG_DV, LANES), lambda b_, i: (b_, 0, 0, 0))
    gn = jnp.tile(gnorm.reshape(1, -1), (1, LANES // HG_DV))
    return pl.pallas_call(
        functools.partial(_hgrn_kernel, c=c, unroll=min(4, c // HG_SUB)),
        grid=(b, t // c),
        in_specs=[tok] * 5 + [stt, pl.BlockSpec((1, LANES), lambda b_, i: (0, 0))],
        out_specs=[tok, stt],
        out_shape=[jax.ShapeDtypeStruct((b, t, HGW), F32), jax.ShapeDtypeStruct((b, hp, HG_DV, LANES), F32)],
        scratch_shapes=[pltpu.VMEM((c, HGW), F32), pltpu.VMEM((hp, HG_DV, LANES), F32)],
        compiler_params=_params(("parallel", "arbitrary")), name="hgrn2",
    )(q, k, v, g, z, s0, gn)


def _state_to_t(s):
    b = s.shape[0]
    return s.reshape(b, HG_HEADS // 2, 2, HG_DK, HG_DV).transpose(0, 1, 4, 2, 3).reshape(b, HG_HEADS // 2, HG_DV, 2 * HG_DK)


def _state_from_t(st):
    b = st.shape[0]
    return st.reshape(b, HG_HEADS // 2, HG_DV, 2, HG_DK).transpose(0, 1, 3, 4, 2).reshape(b, HG_HEADS, HG_DK, HG_DV)


def _rel_bucket_np(rel):
    n = np.maximum(rel, 0)
    exact = REL_BUCKETS // 2
    scaled = np.log(np.maximum(n, 1).astype(np.float32) / exact) / math.log(REL_MAX_DIST / exact)
    large = np.minimum(exact + (np.maximum(scaled, 0.0) * (REL_BUCKETS - exact)).astype(np.int32), REL_BUCKETS - 1)
    return np.where(n < exact, n, large).astype(np.int32)


def _swa_kernel(relb_ref, sink_ref, bucket_ref, q_ref, kp_ref, kc_ref, vp_ref, vc_ref, o_ref, bias_scr,
                *, first_has_no_prev):
    bi = pl.program_id(1)
    w = WINDOW
    nq = q_ref.shape[1]

    @pl.when(jnp.logical_and(pl.program_id(0) == 0, bi == 0))
    def _():
        bucket = bucket_ref[...]
        for h in range(SW_HEADS):
            acc = jnp.zeros((nq, 2 * w), F32)
            for bk in range(REL_BUCKETS):
                acc = jnp.where(bucket == bk, relb_ref[bk, h], acc)
            bias_scr[h] = acc

    qi = lax.broadcasted_iota(jnp.int32, (nq, 2 * w), 0)
    kj = lax.broadcasted_iota(jnp.int32, (nq, 2 * w), 1)
    rel = qi + w - kj
    allowed = jnp.logical_and(rel >= 0, rel <= WINDOW)
    if first_has_no_prev:
        allowed = jnp.logical_and(allowed, jnp.logical_or(bi > 0, kj >= w))
    lane_hi = lax.broadcasted_iota(jnp.int32, (1, LANES), 1) >= SW_DIM
    k = jnp.concatenate([kp_ref[0], kc_ref[0]], axis=0)
    v = jnp.concatenate([vp_ref[0], vc_ref[0]], axis=0)
    ksw = pltpu.roll(k, SW_DIM, 1)
    vsw = pltpu.roll(v, SW_DIM, 1)
    kg = (jnp.where(lane_hi, ksw, k).astype(BF16), jnp.where(lane_hi, k, ksw).astype(BF16))
    vg = (jnp.where(lane_hi, vsw, v).astype(BF16), jnp.where(lane_hi, v, vsw).astype(BF16))
    q = q_ref[0]
    outs = []
    for pair in range(SW_HEADS // 2):
        qb = q[:, pair * LANES:(pair + 1) * LANES]
        zq = jnp.zeros_like(qb)
        res = []
        for half in range(2):
            h = 2 * pair + half
            g = h // SW_GROUP
            qm = jnp.where(lane_hi, qb, zq) if half else jnp.where(lane_hi, zq, qb)
            s = jnp.where(allowed, _dot_nt(qm, kg[g]) + bias_scr[h], MASK_VALUE)
            sink = sink_ref[h]
            m = jnp.maximum(jnp.max(s, axis=-1, keepdims=True), sink)
            e = jnp.exp(s - m)
            den = jnp.sum(e, axis=-1, keepdims=True) + jnp.exp(sink - m)
            res.append(_dot((e / den).astype(BF16), vg[g]))
        outs.append(jnp.where(lane_hi, res[1], res[0]))
    o_ref[0] = jnp.concatenate(outs, axis=1)


def _swa(q, kprev_src, kcur_src, vprev_src, vcur_src, rel_bias, sinks, prev_is_shifted):
    b, s, _ = q.shape
    w = WINDOW
    nb = kcur_src.shape[1] // w
    nq = s // nb
    kvw = SW_KV_HEADS * SW_DIM
    rel = (np.arange(nq)[:, None] + w) - np.arange(2 * w)[None, :]
    bucket = jnp.asarray(_rel_bucket_np(rel))
    cur = pl.BlockSpec((1, w, kvw), lambda b_, i: (b_, i, 0))
    prev = pl.BlockSpec((1, w, kvw), lambda b_, i: (b_, jnp.maximum(i - 1, 0), 0)) if prev_is_shifted else cur
    smem = pl.BlockSpec(memory_space=pltpu.SMEM)
    return pl.pallas_call(
        functools.partial(_swa_kernel, first_has_no_prev=prev_is_shifted),
        grid=(b, nb),
        in_specs=[smem, smem, pl.BlockSpec((nq, 2 * w), lambda b_, i: (0, 0)),
                  pl.BlockSpec((1, nq, SW_HEADS * SW_DIM), lambda b_, i: (b_, i, 0)), prev, cur, prev, cur],
        out_specs=pl.BlockSpec((1, nq, SW_HEADS * SW_DIM), lambda b_, i: (b_, i, 0)),
        out_shape=jax.ShapeDtypeStruct((b, s, SW_HEADS * SW_DIM), F32),
        scratch_shapes=[pltpu.VMEM((SW_HEADS, nq, 2 * w), F32)],
        compiler_params=_params(("arbitrary", "arbitrary")), name="swa",
    )(rel_bias, sinks, bucket, q, kprev_src, kcur_src, vprev_src, vcur_src)


def _tile(m, pref):
    return pref if m % pref == 0 else m


def kernel(x_prompt, x_sample, cache_mla_ckv, cache_mla_krope, cache_sb_k, cache_sb_v, state_hgrn, state_swa_k, state_swa_v, page_table, mix_norm, w_in_even, mla_q_norm, mla_w_qb, mla_kv_norm, mla_w_kvb, w_out_even, w_in_odd, hgrn_lb_logits, hgrn_out_norm, swa_sinks, w_out_odd, rel_bias, mlp_norm, w_up, w_down, final_norm):
    bp, sp, d = x_prompt.shape
    bs, ts, _ = x_sample.shape
    depth = mix_norm.shape[0]
    past = page_table.shape[1] * PAGE_SIZE
    mp, ms = bp * sp, bs * ts
    xp = x_prompt.reshape(mp, d)
    xs = x_sample.reshape(ms, d)
    tab_p = jnp.tile(_rope_tables(jnp.arange(sp)), (bp, 1))
    tab_s = jnp.tile(_rope_tables(past + jnp.arange(ts)), (bs, 1))
    tm_p, tm_s = _tile(mp, 256), _tile(ms, 256)
    t_att = _tile(sp, 256)
    npages = page_table.shape[1]
    pg = next(c for c in (64, 8, 1) if npages % c == 0)
    nbuf = state_swa_k.shape[2]
    hg_pad = HG_SUB - ts
    kvw = SW_KV_HEADS * SW_DIM
    own_head = (jnp.arange(SB_WIDTH)[None, :] // SB_DIM == jnp.arange(SB_HEADS)[:, None])

    outs_p = {k: [] for k in ("ckv", "kr", "sbk", "sbv", "hg", "swk", "swv")}
    outs_s = {k: [] for k in ("ckv", "kr", "sbk", "sbv", "hg", "swk", "swv")}
    for layer in range(depth):
        last = layer == depth - 1
        if layer % 2 == 0:
            e = layer // 2
            wts = _even_weights(w_in_even[e], mla_w_qb[e], mla_w_kvb[e])
            wvbd = wts[6]
            q_p, k_p, v_p, c_p, r_p, sq_p, sk_p, sv_p = _even_proj(
                xp, mix_norm[layer], wts, mla_q_norm[e], mla_kv_norm[e], tab_p, False, tm_p)
            q_s, ql_s, c_s, r_s, sq_s, sk_s, sv_s = _even_proj(
                xs, mix_norm[layer], wts, mla_q_norm[e], mla_kv_norm[e], tab_s, True, tm_s)
            a_p = _mla_prompt(q_p.reshape(bp, sp, -1), k_p.reshape(bp, sp, -1), v_p.reshape(bp, sp, -1),
                              _tile(sp, 512))
            b_p = _sb_prompt(sq_p.reshape(bp, sp, -1), sk_p.reshape(bp, sp, -1), sv_p.reshape(bp, sp, -1), t_att)
            nrow = ts * MLA_HEADS
            qlat = ql_s.reshape(bs, nrow, MLA_KV_LORA)
            qrope = q_s.reshape(bs, nrow, HEAD_PAD)[:, :, MLA_NOPE:MLA_NOPE + MLA_ROPE]
            olat = _mla_sample(page_table, qlat, qrope, c_s.reshape(bs, ts, -1), r_s.reshape(bs, ts, -1),
                               cache_mla_ckv, cache_mla_krope, e, pg)
            a_s = _mm(olat.reshape(ms, MLA_HEADS * MLA_KV_LORA), wvbd, F32)
            qm = jnp.where(own_head[None, None], sq_s.reshape(bs, ts, 1, SB_WIDTH), jnp.zeros((), BF16))
            o_sb = _sb_sample(page_table, qm.reshape(bs, nrow, SB_WIDTH), sk_s.reshape(bs, ts, -1),
                              sv_s.reshape(bs, ts, -1), cache_sb_k, cache_sb_v, e)
            b_s = o_sb.reshape(bs, ts, SB_HEADS, SB_WIDTH).sum(axis=2).reshape(ms, SB_WIDTH)
            a_p, b_p = a_p.reshape(mp, -1), b_p.reshape(mp, -1)
            w_out = w_out_even[e].astype(BF16)
            wa, wb = w_out[:MLA_HEADS * MLA_V], w_out[MLA_HEADS * MLA_V:]
            wa_p = jnp.pad(wa.reshape(MLA_HEADS, MLA_V, d), ((0, 0), (0, HEAD_PAD - MLA_V), (0, 0))).reshape(-1, d)
            outs_p["ckv"].append(c_p.reshape(bp, sp, -1)); outs_p["kr"].append(r_p.reshape(bp, sp, -1))
            outs_p["sbk"].append(sk_p.reshape(bp, sp, SB_HEADS, SB_DIM)); outs_p["sbv"].append(sv_p.reshape(bp, sp, SB_HEADS, SB_DIM))
            outs_s["ckv"].append(c_s.reshape(bs, ts, -1)); outs_s["kr"].append(r_s.reshape(bs, ts, -1))
            outs_s["sbk"].append(sk_s.reshape(bs, ts, SB_HEADS, SB_DIM)); outs_s["sbv"].append(sv_s.reshape(bs, ts, SB_HEADS, SB_DIM))
        else:
            o = layer // 2
            win = w_in_odd[o].astype(BF16)
            lbl = hgrn_lb_logits.astype(F32)
            hq_p, hk_p, hv_p, hg_p, hz_p, dq_p, dk_p, dv_p = _odd_proj(xp, mix_norm[layer], win, lbl, o, tm_p)
            hq_s, hk_s, hv_s, hg_s, hz_s, dq_s, dk_s, dv_s = _odd_proj(xs, mix_norm[layer], win, lbl, o, tm_s)
            r3 = lambda a, b_: a.reshape(b_, -1, a.shape[-1])
            s0_p = jnp.zeros((bp, HG_HEADS // 2, HG_DV, LANES), F32)
            a_p, st_p = _hgrn(r3(hq_p, bp), r3(hk_p, bp), r3(hv_p, bp), r3(hg_p, bp), r3(hz_p, bp), s0_p,
                              hgrn_out_norm[o], _tile(sp, 256))
            padt = lambda a: jnp.pad(r3(a, bs), ((0, 0), (0, hg_pad), (0, 0)))
            a_s, st_s = _hgrn(padt(hq_s), padt(hk_s), padt(hv_s), padt(hg_s), padt(hz_s),
                              _state_to_t(state_hgrn[o].astype(F32)), hgrn_out_norm[o], HG_SUB)
            a_s = a_s[:, :ts].reshape(ms, -1)
            dk3, dv3 = r3(dk_p, bp), r3(dv_p, bp)
            b_p = _swa(r3(dq_p, bp), dk3, dk3, dv3, dv3, rel_bias, swa_sinks[o], True)
            padw = lambda a: jnp.pad(r3(a, bs), ((0, 0), (0, WINDOW - ts), (0, 0)))
            kbuf = state_swa_k[o].reshape(bs, nbuf, kvw)
            vbuf = state_swa_v[o].reshape(bs, nbuf, kvw)
            dq_s8 = jnp.pad(r3(dq_s, bs), ((0, 0), (0, NEW_ROWS - ts), (0, 0)))
            b_s = _swa(dq_s8, kbuf, padw(dk_s), vbuf, padw(dv_s), rel_bias, swa_sinks[o], False)
            b_s = b_s[:, :ts].reshape(ms, -1)
            a_p, b_p = a_p.reshape(mp, -1), b_p.reshape(mp, -1)
            w_out = w_out_odd[o].astype(BF16)
            wa, wb = w_out[:HG_HEADS * HG_DV], w_out[HG_HEADS * HG_DV:]
            wa_p = wa
            nkeep = min(WINDOW, sp)
            outs_p["hg"].append(_state_from_t(st_p)); outs_s["hg"].append(_state_from_t(st_s))
            outs_p["swk"].append(dk3[:, -nkeep:].reshape(bp, nkeep, SW_KV_HEADS, SW_DIM))
            outs_p["swv"].append(dv3[:, -nkeep:].reshape(bp, nkeep, SW_KV_HEADS, SW_DIM))
            kk = jnp.concatenate([kbuf, r3(dk_s, bs)], axis=1)[:, -nbuf:]
            vv = jnp.concatenate([vbuf, r3(dv_s, bs)], axis=1)[:, -nbuf:]
            outs_s["swk"].append(kk.reshape(bs, nbuf, SW_KV_HEADS, SW_DIM))
            outs_s["swv"].append(vv.reshape(bs, nbuf, SW_KV_HEADS, SW_DIM))
        wu, wd = w_up[layer].astype(BF16), w_down[layer].astype(BF16)
        xp = _post(xp, a_p, b_p, wa_p, wb, mlp_norm[layer], wu, wd, final_norm, last, _tile(mp, 512), _tile(D_FF, 1024))
        xs = _post(xs, a_s, b_s, wa, wb, mlp_norm[layer], wu, wd, final_norm, last, _tile(ms, 512), _tile(D_FF, 1024))
    st = lambda lst: jnp.stack(lst, axis=0)
    return (xp.reshape(bp, sp, d), xs.reshape(bs, ts, d),
            st(outs_p["ckv"]), st(outs_p["kr"]), st(outs_p["sbk"]), st(outs_p["sbv"]), st(outs_p["hg"]),
            st(outs_p["swk"]), st(outs_p["swv"]),
            st(outs_s["ckv"]), st(outs_s["kr"]), st(outs_s["sbk"]), st(outs_s["sbv"]), st(outs_s["hg"]),
            st(outs_s["swk"]), st(outs_s["swv"]))
```

```python
import functools
import math

import numpy as np
import jax
import jax.numpy as jnp
from jax import lax
from jax.experimental import pallas as pl
from jax.experimental.pallas import tpu as pltpu

F32 = jnp.float32
BF16 = jnp.bfloat16

D_MODEL = 1024
PAGE_SIZE = 128
MLA_HEADS = 8
MLA_NOPE = 64
MLA_ROPE = 32
MLA_V = 64
MLA_Q_LORA = 384
MLA_KV_LORA = 256
MLA_SCALE = (MLA_NOPE + MLA_ROPE) ** -0.5
MLA_QSCALE = MLA_SCALE * math.log2(math.e)
ROPE_THETA = 10000.0
SB_HEADS = 8
SB_DIM = 64
SB_WIDTH = SB_HEADS * SB_DIM
SB_SCALE = SB_DIM ** -0.5
HG_HEADS = 8
HG_DK = 64
HG_DV = 64
HG_SUB = 16
FORGET_FLOOR = 1e-30
SW_HEADS = 8
SW_KV_HEADS = 2
SW_GROUP = SW_HEADS // SW_KV_HEADS
SW_DIM = 64
SW_SCALE = SW_DIM ** -0.5
WINDOW = 128
REL_BUCKETS = 32
REL_MAX_DIST = 128
D_FF = 4 * D_MODEL
NORM_EPS = 1e-6
MASK_VALUE = -1e30
SB_EXIT = -150.0

LANES = 128
HEAD_PAD = 128
VMEM_LIMIT = 56 * 1024 * 1024

NT = (((1,), (1,)), ((), ()))


def _dot(a, b):
    return jnp.dot(a, b, preferred_element_type=F32)


def _dot_nt(a, b):
    return lax.dot_general(a, b, NT, preferred_element_type=F32)


def _rms(x, g):
    return x * lax.rsqrt(jnp.mean(x * x, axis=-1, keepdims=True) + NORM_EPS) * g


def _split2(x):
    hi = x.astype(BF16)
    lo = (x - hi.astype(F32)).astype(BF16)
    return hi, lo


def _params(sem):
    return pltpu.CompilerParams(dimension_semantics=sem, vmem_limit_bytes=VMEM_LIMIT)


def _full(shape):
    n = len(shape)
    return pl.BlockSpec(shape, lambda *_: (0,) * n)


EV_QA = 0
EV_KVA = MLA_Q_LORA
EV_SQ = EV_KVA + MLA_KV_LORA
EV_SK = EV_SQ + SB_WIDTH
EV_SV = EV_SK + SB_WIDTH
EV_KR = EV_SV + SB_WIDTH
EV_KRS = EV_KR + LANES
EV_N = EV_KRS + LANES


def _even_common(x_ref, g_ref, win_ref, qn_ref, wq_ref, wqs_ref, kvn_ref, tab_ref):
    h = _rms(x_ref[...], g_ref[...]).astype(BF16)
    hh = _dot(h, win_ref[...])
    tab = tab_ref[...]
    cos_q, sin_q = tab[:, 0:LANES], tab[:, LANES:2 * LANES]
    cos_k, sin_k = tab[:, 2 * LANES:3 * LANES], tab[:, 3 * LANES:4 * LANES]
    qa = _rms(hh[:, EV_QA:EV_KVA], qn_ref[...]).astype(BF16)
    q = (_dot(qa, wq_ref[...]) * jnp.tile(cos_q, (1, MLA_HEADS))
         + _dot(qa, wqs_ref[...]) * jnp.tile(sin_q, (1, MLA_HEADS)))
    ckv = _rms(hh[:, EV_KVA:EV_SQ], kvn_ref[...])
    kr = hh[:, EV_KR:EV_KRS] * cos_k + hh[:, EV_KRS:EV_N] * sin_k
    sq = (hh[:, EV_SQ:EV_SK] * SB_SCALE).astype(BF16)
    return q, ckv, kr, sq, hh[:, EV_SK:EV_SV], hh[:, EV_SV:EV_KR]


def _even_prompt_kernel(x_ref, g_ref, win_ref, qn_ref, wq_ref, wqs_ref, kvn_ref, tab_ref,
                        wk_ref, wv_ref, e_ref,
                        q_out, k_out, v_out, ckv_out, kr_out, sq_out, sk_out, sv_out):
    q, ckv, kr, sq, sk, sv = _even_common(x_ref, g_ref, win_ref, qn_ref, wq_ref, wqs_ref, kvn_ref, tab_ref)
    q_out[...] = q.astype(BF16)
    ckv_out[...] = ckv
    kr_out[...] = kr[:, :MLA_ROPE]
    sq_out[...] = sq
    sk_out[...] = sk
    sv_out[...] = sv
    cb = ckv.astype(BF16)
    k_out[...] = (_dot(cb, wk_ref[...]) + _dot(kr.astype(BF16), e_ref[...])).astype(BF16)
    lane = lax.broadcasted_iota(jnp.int32, (1, MLA_HEADS * HEAD_PAD), 1)
    ones_lane = jnp.where(lane % HEAD_PAD == MLA_V, 1.0, 0.0)
    v_out[...] = (_dot(cb, wv_ref[...]) + ones_lane).astype(BF16)


def _even_sample_kernel(x_ref, g_ref, win_ref, qn_ref, wq_ref, wqs_ref, kvn_ref, tab_ref,
                        wabs_ref,
                        q_out, qlat_out, ckv_out, kr_out, sq_out, sk_out, sv_out):
    q, ckv, kr, sq, sk, sv = _even_common(x_ref, g_ref, win_ref, qn_ref, wq_ref, wqs_ref, kvn_ref, tab_ref)
    qb = q.astype(BF16)
    q_out[...] = qb
    qlat_out[...] = _dot(qb, wabs_ref[...]).astype(BF16)
    ckv_out[...] = ckv
    kr_out[...] = kr[:, :MLA_ROPE]
    sq_out[...] = sq
    sk_out[...] = sk
    sv_out[...] = sv


def _rope_tables(pos):
    half = MLA_ROPE // 2
    inv = ROPE_THETA ** (-jnp.arange(half, dtype=F32) / half)
    ang = pos.astype(F32)[:, None] * inv[None, :]
    cos = jnp.cos(ang)
    sin = jnp.sin(ang)
    cos2 = jnp.concatenate([cos, cos], axis=-1)
    sin2 = jnp.concatenate([sin, sin], axis=-1)
    n = pos.shape[0]
    one = jnp.ones((n, MLA_NOPE), F32)
    z = lambda w: jnp.zeros((n, w), F32)
    cos_q = jnp.concatenate([one, cos2, z(HEAD_PAD - MLA_NOPE - MLA_ROPE)], -1) * MLA_QSCALE
    sin_q = jnp.concatenate([z(MLA_NOPE), sin2, z(HEAD_PAD - MLA_NOPE - MLA_ROPE)], -1) * MLA_QSCALE
    cos_k = jnp.concatenate([cos2, z(LANES - MLA_ROPE)], -1)
    sin_k = jnp.concatenate([sin2, z(LANES - MLA_ROPE)], -1)
    return jnp.concatenate([cos_q, sin_q, cos_k, sin_k], -1)


def _rot_cols(w):
    half = w.shape[-1] // 2
    return jnp.concatenate([-w[..., half:], w[..., :half]], axis=-1)


def _even_weights(w_in, w_qb, w_kvb):
    d = w_in.shape[0]
    o_kva, o_kr = MLA_Q_LORA, MLA_Q_LORA + MLA_KV_LORA
    o_sq = o_kr + MLA_ROPE
    w_kr = w_in[:, o_kr:o_sq]
    zpad = jnp.zeros((d, LANES - MLA_ROPE), F32)
    win = jnp.concatenate([w_in[:, :o_kr], w_in[:, o_sq:], w_kr, zpad, _rot_cols(w_kr), zpad], -1).astype(BF16)
    wq3 = w_qb.reshape(MLA_Q_LORA, MLA_HEADS, MLA_NOPE + MLA_ROPE)
    nope, rp = wq3[..., :MLA_NOPE], wq3[..., MLA_NOPE:]
    zq = lambda w: jnp.zeros((MLA_Q_LORA, MLA_HEADS, w), F32)
    pad = HEAD_PAD - MLA_NOPE - MLA_ROPE
    wq = jnp.concatenate([nope, rp, zq(pad)], -1).reshape(MLA_Q_LORA, -1).astype(BF16)
    wqs = jnp.concatenate([zq(MLA_NOPE), _rot_cols(rp), zq(pad)], -1).reshape(MLA_Q_LORA, -1).astype(BF16)
    wkv3 = w_kvb.reshape(MLA_KV_LORA, MLA_HEADS, MLA_NOPE + MLA_V)
    knope, wv3 = wkv3[..., :MLA_NOPE], wkv3[..., MLA_NOPE:]
    wk = jnp.concatenate([knope, jnp.zeros((MLA_KV_LORA, MLA_HEADS, HEAD_PAD - MLA_NOPE), F32)], -1)
    wk = wk.reshape(MLA_KV_LORA, -1).astype(BF16)
    wv = jnp.concatenate([wv3, jnp.zeros((MLA_KV_LORA, MLA_HEADS, HEAD_PAD - MLA_V), F32)], -1)
    wv = wv.reshape(MLA_KV_LORA, -1).astype(BF16)
    eye = jnp.eye(MLA_HEADS, dtype=F32)
    kt = jnp.concatenate([knope, jnp.zeros((MLA_KV_LORA, MLA_HEADS, HEAD_PAD - MLA_NOPE), F32)], -1)
    wabs = jnp.einsum('chn,hg->hngc', kt, eye).reshape(MLA_HEADS * HEAD_PAD, MLA_HEADS * MLA_KV_LORA).astype(BF16)
    wvbd = jnp.einsum('chv,hg->hcgv', wv3, eye).reshape(MLA_HEADS * MLA_KV_LORA, MLA_HEADS * MLA_V).astype(BF16)
    return win, wq, wqs, wk, wv, wabs, wvbd


def _krope_place():
    e = np.zeros((LANES, MLA_HEADS * HEAD_PAD), np.float32)
    for h in range(MLA_HEADS):
        for i in range(MLA_ROPE):
            e[i, h * HEAD_PAD + MLA_NOPE + i] = 1.0
    return jnp.asarray(e, BF16)


def _even_proj(x, gain, weights, q_norm, kv_norm, tab, sample, tm):
    m = x.shape[0]
    win, wq, wqs, wk, wv, wabs, _ = weights
    hq = MLA_HEADS * HEAD_PAD
    row = lambda w: pl.BlockSpec((tm, w), lambda i: (i, 0))
    common_in = [row(D_MODEL), _full((1, D_MODEL)), _full(win.shape), _full((1, MLA_Q_LORA)), _full(wq.shape),
                 _full(wqs.shape), _full((1, MLA_KV_LORA)), row(4 * LANES)]
    common_args = [x, gain.reshape(1, -1), win, q_norm.reshape(1, -1), wq, wqs, kv_norm.reshape(1, -1), tab]
    sds = jax.ShapeDtypeStruct
    tail_shapes = [sds((m, MLA_KV_LORA), F32), sds((m, MLA_ROPE), F32), sds((m, SB_WIDTH), BF16),
                   sds((m, SB_WIDTH), F32), sds((m, SB_WIDTH), F32)]
    tail_specs = [row(MLA_KV_LORA), row(MLA_ROPE), row(SB_WIDTH), row(SB_WIDTH), row(SB_WIDTH)]
    if sample:
        return pl.pallas_call(
            _even_sample_kernel, grid=(m // tm,),
            in_specs=common_in + [_full(wabs.shape)],
            out_specs=[row(hq), row(MLA_HEADS * MLA_KV_LORA)] + tail_specs,
            out_shape=[sds((m, hq), BF16), sds((m, MLA_HEADS * MLA_KV_LORA), BF16)] + tail_shapes,
            compiler_params=_params(("parallel",)), name="even_proj_sample",
        )(*common_args, wabs)
    e = _krope_place()
    return pl.pallas_call(
        _even_prompt_kernel, grid=(m // tm,),
        in_specs=common_in + [_full(wk.shape), _full(wv.shape), _full(e.shape)],
        out_specs=[row(hq), row(hq), row(hq)] + tail_specs,
        out_shape=[sds((m, hq), BF16), sds((m, hq), BF16), sds((m, hq), BF16)] + tail_shapes,
        compiler_params=_params(("parallel",)), name="even_proj_prompt",
    )(*common_args, wk, wv, e)


def _mla_prompt_kernel(q_ref, k_ref, v_ref, o_ref, *, t, rt):
    qi = pl.program_id(2)
    nsub = t // rt
    qs = [q_ref[0, r * rt:(r + 1) * rt, :] for r in range(nsub)]
    rows = lax.broadcasted_iota(jnp.int32, (rt, t), 0)
    cols = lax.broadcasted_iota(jnp.int32, (rt, t), 1)

    def block(jb, carry, diag):
        ks = pl.multiple_of(jb * t, t)
        k = k_ref[0, pl.ds(ks, t), :]
        v = v_ref[0, pl.ds(ks, t), :]
        ss = [_dot_nt(qs[r], k) for r in range(nsub)]
        if diag:
            ss = [jnp.where(cols <= rows + r * rt, ss[r], MASK_VALUE) for r in range(nsub)]
        mns = [jnp.maximum(carry[2 * r], jnp.max(ss[r], axis=-1, keepdims=True)) for r in range(nsub)]
        ps = [jnp.exp2(ss[r] - mns[r]).astype(BF16) for r in range(nsub)]
        out = []
        for r in range(nsub):
            out += [mns[r], jnp.exp2(carry[2 * r] - mns[r]) * carry[2 * r + 1] + _dot(ps[r], v)]
        return tuple(out)

    init = (jnp.full((rt, 1), MASK_VALUE, F32), jnp.zeros((rt, HEAD_PAD), F32)) * nsub
    carry = lax.fori_loop(0, qi, lambda jb, c: block(jb, c, False), init)
    carry = block(qi, carry, True)
    for r in range(nsub):
        acc = carry[2 * r + 1]
        o_ref[0, r * rt:(r + 1) * rt, :] = acc / acc[:, MLA_V:MLA_V + 1]


def _mla_prompt(q, k, v, t):
    b, s, _ = q.shape
    tile = pl.BlockSpec((1, t, HEAD_PAD), lambda b_, h, i: (b_, i, h))
    seq = pl.BlockSpec((1, s, HEAD_PAD), lambda b_, h, i: (b_, 0, h))
    return pl.pallas_call(
        functools.partial(_mla_prompt_kernel, t=t, rt=min(t, 128)),
        grid=(b, MLA_HEADS, s // t),
        in_specs=[tile, seq, seq],
        out_specs=tile,
        out_shape=jax.ShapeDtypeStruct((b, s, MLA_HEADS * HEAD_PAD), F32),
        compiler_params=_params(("parallel", "parallel", "arbitrary")), name="mla_prompt",
    )(q, k, v)


def _log_sig_neg(z):
    return -(jnp.maximum(z, 0.0) + jnp.log(1.0 + jnp.exp(-jnp.abs(z))))


def _suffix_tri(n):
    return (lax.broadcasted_iota(jnp.int32, (n, n), 0) >= lax.broadcasted_iota(jnp.int32, (n, n), 1)).astype(BF16)


def _sb_blocks(qhs, kb, vb, tri, carries, alloweds, key_minor=False):
    n = len(qhs)
    zs = [_dot(qh, kb) if key_minor else _dot_nt(qh, kb) for qh in qhs]
    lsms = [_log_sig_neg(z) for z in zs]
    lsms = [l if a is None else jnp.where(a, l, 0.0) for l, a in zip(lsms, alloweds)]
    parts = [_split2(l) for l in lsms]
    css = [_dot(hi, tri) + _dot(lo, tri) for hi, lo in parts]
    las = [zs[i] + css[i] + carries[i] for i in range(n)]
    las = [l if a is None else jnp.where(a, l, MASK_VALUE) for l, a in zip(las, alloweds)]
    ps = [jnp.exp(l).astype(BF16) for l in las]
    pvs = [_dot_nt(p, vb) if key_minor else _dot(p, vb) for p in ps]
    return pvs, [carries[i] + css[i][:, :1] for i in range(n)]


def _sb_block(qh, kb, vb, tri, carry, allowed, key_minor=False):
    pvs, cs = _sb_blocks([qh], kb, vb, tri, [carry], [allowed], key_minor)
    return pvs[0], cs[0]


def _sb_prompt_kernel(q_ref, k_ref, v_ref, o_ref, *, t, rt):
    qi = pl.program_id(2)
    nsub = t // rt
    head1 = lax.broadcasted_iota(jnp.int32, (1, LANES), 1) >= SB_DIM
    qhs = []
    for r in range(nsub):
        q = q_ref[0, r * rt:(r + 1) * rt, :]
        zq = jnp.zeros_like(q)
        qhs += [jnp.where(head1, zq, q), jnp.where(head1, q, zq)]
    tri = _suffix_tri(t)
    rows = lax.broadcasted_iota(jnp.int32, (rt, t), 0)
    cols = lax.broadcasted_iota(jnp.int32, (rt, t), 1)
    strict = [cols < rows + (i // 2) * rt for i in range(2 * nsub)]

    def block(jb, cs, accs, alloweds):
        ks = pl.multiple_of(jb * t, t)
        kb = k_ref[0, pl.ds(ks, t), :].astype(BF16)
        vb = v_ref[0, pl.ds(ks, t), :].astype(BF16)
        pvs, cs = _sb_blocks(qhs, kb, vb, tri, cs, alloweds)
        accs = [accs[r] + jnp.where(head1, pvs[2 * r + 1], pvs[2 * r]) for r in range(nsub)]
        return cs, accs

    cmax = lambda cs: jnp.max(functools.reduce(jnp.maximum, cs))
    cs, accs = block(qi, [jnp.zeros((rt, 1), F32)] * (2 * nsub), [jnp.zeros((rt, LANES), F32)] * nsub, strict)

    def cond(st):
        return jnp.logical_and(st[0] >= 0, st[3] > SB_EXIT)

    def body(st):
        jb, cs, accs, _ = st
        cs, accs = block(jb, list(cs), list(accs), [None] * (2 * nsub))
        return jb - 1, tuple(cs), tuple(accs), cmax(cs)

    st = lax.while_loop(cond, body, (qi - 1, tuple(cs), tuple(accs), cmax(cs)))
    for r in range(nsub):
        o_ref[0, r * rt:(r + 1) * rt, :] = st[2][r]


def _sb_prompt(q, k, v, t):
    b, s, _ = q.shape
    hp = SB_HEADS // 2
    spec_q = pl.BlockSpec((1, t, LANES), lambda b_, h, i: (b_, i, h))
    spec_kv = pl.BlockSpec((1, s, LANES), lambda b_, h, i: (b_, 0, h))
    return pl.pallas_call(
        functools.partial(_sb_prompt_kernel, t=t, rt=min(t, 128)),
        grid=(b, hp, s // t),
        in_specs=[spec_q, spec_kv, spec_kv],
        out_specs=spec_q,
        out_shape=jax.ShapeDtypeStruct((b, s, SB_WIDTH), F32),
        compiler_params=_params(("parallel", "parallel", "arbitrary")), name="sb_prompt",
    )(q, k, v)


def _mla_sample_kernel(pt_ref, qlat_ref, qr_ref, cnew_ref, rnew_ref, *rest, pg, nt, ng):
    ckv_refs = rest[:pg]
    kr_refs = rest[pg:2 * pg]
    o_ref, kbuf, rbuf, m_scr, l_scr, acc_scr = rest[2 * pg:]
    g = pl.program_id(1)
    ql = qlat_ref[0]
    qr = qr_ref[0]
    nrow = ql.shape[0]

    def update(s, vals):
        m = m_scr[...]
        mn = jnp.maximum(m, jnp.max(s, axis=-1, keepdims=True))
        a = jnp.exp2(m - mn)
        p = jnp.exp2(s - mn)
        l_scr[...] = a * l_scr[...] + jnp.sum(p, axis=-1, keepdims=True)
        acc_scr[...] = a * acc_scr[...] + _dot(p.astype(BF16), vals)
        m_scr[...] = mn

    @pl.when(g == 0)
    def _():
        m_scr[...] = jnp.full(m_scr.shape, MASK_VALUE, F32)
        l_scr[...] = jnp.zeros(l_scr.shape, F32)
        acc_scr[...] = jnp.zeros(acc_scr.shape, F32)
        cn = cnew_ref[0].astype(BF16)
        s = _dot_nt(ql, cn) + _dot_nt(qr, rnew_ref[0].astype(BF16))
        tq = lax.broadcasted_iota(jnp.int32, s.shape, 0) // MLA_HEADS
        sk = lax.broadcasted_iota(jnp.int32, s.shape, 1)
        s = jnp.where(jnp.logical_and(sk <= tq, sk < nt), s, MASK_VALUE)
        update(s, cn)

    for i in range(pg):
        kbuf[i * PAGE_SIZE:(i + 1) * PAGE_SIZE, :] = ckv_refs[i][...].astype(BF16)
        rbuf[:, i * PAGE_SIZE:(i + 1) * PAGE_SIZE] = kr_refs[i][...].astype(BF16)
    kb = kbuf[...]
    update(_dot_nt(ql, kb) + _dot(qr, rbuf[...]), kb)

    @pl.when(g == ng - 1)
    def _():
        o_ref[0] = acc_scr[...] / l_scr[...]


def _mla_sample(page_table, qlat, qr, cnew, rnew, cache_ckv, cache_kr, layer, pg):
    b, nrow, _ = qlat.shape
    npages = page_table.shape[1]
    ng = npages // pg
    nt = nrow // MLA_HEADS
    pad = PAGE_SIZE - cnew.shape[1]
    cnew = jnp.pad(cnew, ((0, 0), (0, pad), (0, 0)))
    rnew = jnp.pad(rnew, ((0, 0), (0, pad), (0, 0)))

    cache_kr_t = jnp.swapaxes(cache_kr, 2, 3)

    def page_spec(i, rows, width):
        return pl.BlockSpec((None, None, rows, width),
                            lambda b_, g, pt: (layer, pt[b_ * npages + g * pg + i], 0, 0))

    seq = lambda r, w: pl.BlockSpec((1, r, w), lambda b_, g, pt: (b_, 0, 0))
    grid_spec = pltpu.PrefetchScalarGridSpec(
        num_scalar_prefetch=1, grid=(b, ng),
        in_specs=[seq(nrow, MLA_KV_LORA), seq(nrow, MLA_ROPE), seq(PAGE_SIZE, MLA_KV_LORA), seq(PAGE_SIZE, MLA_ROPE)]
        + [page_spec(i, PAGE_SIZE, MLA_KV_LORA) for i in range(pg)]
        + [page_spec(i, MLA_ROPE, PAGE_SIZE) for i in range(pg)],
        out_specs=seq(nrow, MLA_KV_LORA),
        scratch_shapes=[pltpu.VMEM((pg * PAGE_SIZE, MLA_KV_LORA), BF16), pltpu.VMEM((MLA_ROPE, pg * PAGE_SIZE), BF16),
                        pltpu.VMEM((nrow, 1), F32), pltpu.VMEM((nrow, 1), F32), pltpu.VMEM((nrow, MLA_KV_LORA), F32)])
    return pl.pallas_call(
        functools.partial(_mla_sample_kernel, pg=pg, nt=nt, ng=ng),
        grid_spec=grid_spec,
        out_shape=jax.ShapeDtypeStruct((b, nrow, MLA_KV_LORA), F32),
        compiler_params=_params(("parallel", "arbitrary")), name="mla_sample",
    )(page_table.reshape(-1), qlat, qr, cnew, rnew, *([cache_ckv] * pg), *([cache_kr_t] * pg))


NEW_ROWS = 8


def _sb_sample_kernel(pt_ref, q_ref, knew_ref, vnew_ref, ck_hbm, cv_hbm, o_ref, kbuf, vbuf, knpad, vnpad, sem,
                      *, layer, npages, nt):
    b = pl.program_id(0)
    q = q_ref[0]
    nrow = q.shape[0]
    tri = _suffix_tri(PAGE_SIZE)

    def copies(p, slot):
        page = pt_ref[b * npages + p]
        return (pltpu.make_async_copy(ck_hbm.at[layer, page], kbuf.at[slot], sem.at[0, slot]),
                pltpu.make_async_copy(cv_hbm.at[layer, page], vbuf.at[slot], sem.at[1, slot]))

    def start(p, slot):
        for c in copies(p, slot):
            c.start()

    def wait(p, slot):
        for c in copies(p, slot):
            c.wait()

    slot_of = lambda p: lax.rem(npages - 1 - p, 2)
    start(npages - 1, 0)

    zpad = jnp.zeros((PAGE_SIZE - NEW_ROWS, SB_WIDTH), F32)
    knpad[...] = jnp.concatenate([knew_ref[0], zpad], axis=0)
    vnpad[...] = jnp.concatenate([vnew_ref[0], zpad], axis=0)
    tq = lax.broadcasted_iota(jnp.int32, (nrow, PAGE_SIZE), 0) // SB_HEADS
    sk = lax.broadcasted_iota(jnp.int32, (nrow, PAGE_SIZE), 1)
    acc, c = _sb_block(q, knpad[...].astype(BF16), vnpad[...].astype(BF16), tri, jnp.zeros((nrow, 1), F32),
                       jnp.logical_and(sk < tq, sk < nt))

    def cond(st):
        return jnp.logical_and(st[0] >= 0, st[3] > SB_EXIT)

    def body(st):
        p, c, acc, _ = st
        slot = slot_of(p)
        wait(p, slot)

        @pl.when(p > 0)
        def _():
            start(p - 1, 1 - slot)

        pv, c = _sb_block(q, kbuf[slot].astype(BF16), vbuf[slot].astype(BF16), tri, c, None, key_minor=True)
        return p - 1, c, acc + pv, jnp.max(c)

    p, _, acc, _ = lax.while_loop(cond, body, (jnp.int32(npages - 1), c, acc, jnp.max(c)))

    @pl.when(p >= 0)
    def _():
        wait(p, slot_of(p))

    own = (lax.broadcasted_iota(jnp.int32, (nrow, SB_WIDTH), 1) // SB_DIM
           == lax.broadcasted_iota(jnp.int32, (nrow, SB_WIDTH), 0) % SB_HEADS)
    o_ref[0] = jnp.where(own, acc, 0.0)


def _sb_sample(page_table, qm, knew, vnew, cache_k, cache_v, layer):
    b, nrow, _ = qm.shape
    npages = page_table.shape[1]
    nt = nrow // SB_HEADS
    pad = NEW_ROWS - knew.shape[1]
    knew = jnp.pad(knew, ((0, 0), (0, pad), (0, 0)))
    vnew = jnp.pad(vnew, ((0, 0), (0, pad), (0, 0)))
    key_minor = lambda c: jnp.transpose(c, (0, 1, 3, 4, 2)).reshape(c.shape[0], c.shape[1], SB_WIDTH, PAGE_SIZE)
    seq = lambda r: pl.BlockSpec((1, r, SB_WIDTH), lambda b_, pt: (b_, 0, 0))
    hbm = pl.BlockSpec(memory_space=pl.ANY)
    page_buf = pltpu.VMEM((2, SB_WIDTH, PAGE_SIZE), F32)
    new_buf = pltpu.VMEM((PAGE_SIZE, SB_WIDTH), F32)
    grid_spec = pltpu.PrefetchScalarGridSpec(
        num_scalar_prefetch=1, grid=(b,),
        in_specs=[seq(nrow), seq(NEW_ROWS), seq(NEW_ROWS), hbm, hbm],
        out_specs=seq(nrow),
        scratch_shapes=[page_buf, page_buf, new_buf, new_buf, pltpu.SemaphoreType.DMA((2, 2))])
    return pl.pallas_call(
        functools.partial(_sb_sample_kernel, layer=layer, npages=npages, nt=nt),
        grid_spec=grid_spec,
        out_shape=jax.ShapeDtypeStruct((b, nrow, SB_WIDTH), F32),
        compiler_params=_params(("arbitrary",)), name="sb_sample",
    )(page_table.reshape(-1), qm, knew, vnew, key_minor(cache_k), key_minor(cache_v))


def _mm_kernel(x_ref, w_ref, o_ref):
    o_ref[...] = _dot(x_ref[...].astype(BF16), w_ref[...]).astype(o_ref.dtype)


def _mm(x, w, out_dtype):
    m, n = x.shape[0], w.shape[1]
    return pl.pallas_call(
        _mm_kernel, grid=(1,), in_specs=[_full(x.shape), _full(w.shape)], out_specs=_full((m, n)),
        out_shape=jax.ShapeDtypeStruct((m, n), out_dtype), compiler_params=_params(("arbitrary",)), name="mm",
    )(x, w)


def _post_kernel(x_ref, a_ref, b_ref, wa_ref, wb_ref, g_ref, wu_ref, wd_ref, fg_ref, o_ref,
                 x1_scr, h_scr, acc_scr, *, final):
    j = pl.program_id(1)

    @pl.when(j == 0)
    def _():
        x1 = (x_ref[...] + _dot(a_ref[...].astype(BF16), wa_ref[...])
              + _dot(b_ref[...].astype(BF16), wb_ref[...]))
        x1_scr[...] = x1
        h_scr[...] = _rms(x1, g_ref[...]).astype(BF16)
        acc_scr[...] = jnp.zeros(acc_scr.shape, F32)

    u = jnp.maximum(_dot(h_scr[...], wu_ref[...]), 0.0)
    acc_scr[...] += _dot((u * u).astype(BF16), wd_ref[...])

    @pl.when(j == pl.num_programs(1) - 1)
    def _():
        y = x1_scr[...] + acc_scr[...]
        o_ref[...] = _rms(y, fg_ref[...]) if final else y


def _post(x, a, b, wa, wb, gain, wu, wd, fgain, final, tm, tf):
    m = x.shape[0]
    ff = wu.shape[1]
    row = lambda w: pl.BlockSpec((tm, w), lambda i, j: (i, 0))
    const = lambda shape: pl.BlockSpec(shape, lambda i, j: (0, 0))
    return pl.pallas_call(
        functools.partial(_post_kernel, final=final),
        grid=(m // tm, ff // tf),
        in_specs=[row(D_MODEL), row(a.shape[1]), row(b.shape[1]), const(wa.shape), const(wb.shape),
                  const((1, D_MODEL)), pl.BlockSpec((D_MODEL, tf), lambda i, j: (0, j)),
                  pl.BlockSpec((tf, D_MODEL), lambda i, j: (j, 0)), const((1, D_MODEL))],
        out_specs=row(D_MODEL),
        out_shape=jax.ShapeDtypeStruct((m, D_MODEL), F32),
        scratch_shapes=[pltpu.VMEM((tm, D_MODEL), F32), pltpu.VMEM((tm, D_MODEL), BF16),
                        pltpu.VMEM((tm, D_MODEL), F32)],
        compiler_params=_params(("parallel", "arbitrary")), name="post_mlp",
    )(x, a, b, wa, wb, gain.reshape(1, -1), wu, wd, fgain.reshape(1, -1))


HGW = HG_HEADS * HG_DK
OD_Q, OD_F, OD_I, OD_G = 0, HGW, 2 * HGW, 3 * HGW
OD_DQ = 4 * HGW
OD_DK = OD_DQ + SW_HEADS * SW_DIM
OD_DV = OD_DK + SW_KV_HEADS * SW_DIM
OD_N = OD_DV + SW_KV_HEADS * SW_DIM


def _odd_proj_kernel(x_ref, g_ref, win_ref, lbl_ref, q_out, k_out, v_out, lf_out, z_out, dq_out, dk_out, dv_out,
                     *, layer):
    h = _rms(x_ref[...], g_ref[...]).astype(BF16)
    hh = _dot(h, win_ref[...])
    lbl = lbl_ref[...]
    ex = jnp.exp(lbl - jnp.max(lbl, axis=0, keepdims=True))
    pr = ex / jnp.sum(ex, axis=0, keepdims=True)
    lb = jnp.zeros((1, HGW), F32)
    for i in range(1, layer + 1):
        lb = lb + pr[i:i + 1, :]
    hq = hh[:, OD_Q:OD_F]
    f = lb + (1.0 - lb) * jax.nn.sigmoid(hh[:, OD_F:OD_I])
    fc = jnp.maximum(f, FORGET_FLOOR)
    q_out[...] = hq * jax.nn.sigmoid(hq)
    k_out[...] = 1.0 - fc
    v_out[...] = hh[:, OD_I:OD_G]
    lf_out[...] = jnp.log(fc)
    hz = hh[:, OD_G:OD_DQ]
    z_out[...] = hz * jax.nn.sigmoid(hz)
    dq_out[...] = (hh[:, OD_DQ:OD_DK] * SW_SCALE).astype(BF16)
    dk_out[...] = hh[:, OD_DK:OD_DV]
    dv_out[...] = hh[:, OD_DV:OD_N]


def _odd_proj(x, gain, win, lb_logits, layer, tm):
    m = x.shape[0]
    row = lambda w: pl.BlockSpec((tm, w), lambda i: (i, 0))
    sds = jax.ShapeDtypeStruct
    kvw = SW_KV_HEADS * SW_DIM
    return pl.pallas_call(
        functools.partial(_odd_proj_kernel, layer=layer), grid=(m // tm,),
        in_specs=[row(D_MODEL), _full((1, D_MODEL)), _full(win.shape), _full(lb_logits.shape)],
        out_specs=[row(HGW)] * 5 + [row(SW_HEADS * SW_DIM), row(kvw), row(kvw)],
        out_shape=[sds((m, HGW), F32)] * 5 + [sds((m, SW_HEADS * SW_DIM), BF16), sds((m, kvw), F32), sds((m, kvw), F32)],
        compiler_params=_params(("parallel",)), name="odd_proj",
    )(x, gain.reshape(1, -1), win, lb_logits)


def _hgrn_kernel(q_ref, k_ref, v_ref, g_ref, z_ref, s0_ref, gn_ref, y_ref, sout_ref, gc_scr, st_scr, *, c, unroll):
    ci = pl.program_id(1)
    nsub = c // HG_SUB
    npair = HG_HEADS // 2

    @pl.when(ci == 0)
    def _():
        st_scr[...] = s0_ref[0]

    r = lax.broadcasted_iota(jnp.int32, (c, c), 0)
    cc = lax.broadcasted_iota(jnp.int32, (c, c), 1)
    lseg = jnp.logical_and(r >= cc, r // HG_SUB == cc // HG_SUB).astype(BF16)
    g = g_ref[0]
    g1 = g.astype(BF16)
    r1 = g - g1.astype(F32)
    g2 = r1.astype(BF16)
    g3 = (r1 - g2.astype(F32)).astype(BF16)
    gc_scr[...] = _dot(lseg, g1) + _dot(lseg, g2) + _dot(lseg, g3)

    same_head = (lax.broadcasted_iota(jnp.int32, (LANES, LANES), 0) // HG_DV
                 == lax.broadcasted_iota(jnp.int32, (LANES, LANES), 1) // HG_DK)
    ones_bd = same_head.astype(BF16)
    rows = lax.broadcasted_iota(jnp.int32, (HG_SUB, LANES), 0)
    gn = gn_ref[...]

    def sub_pair(off, pr):
        lanes = slice(pr * LANES, (pr + 1) * LANES)
        q = q_ref[0, pl.ds(off, HG_SUB), lanes]
        k = k_ref[0, pl.ds(off, HG_SUB), lanes]
        v = v_ref[0, pl.ds(off, HG_SUB), lanes]
        z = z_ref[0, pl.ds(off, HG_SUB), lanes]
        gc = gc_scr[pl.ds(off, HG_SUB), lanes]
        st = st_scr[pr]
        xs = []
        for s in range(HG_SUB):
            d = jnp.exp(jnp.minimum(gc - gc[s:s + 1, :], 0.0))
            xs.append(jnp.where(rows >= s, q * k[s:s + 1, :] * d, 0.0))
        x = jnp.concatenate(xs, axis=0).astype(BF16)
        a = _dot(x, ones_bd)
        o = jnp.zeros((HG_SUB, LANES), F32)
        for s in range(HG_SUB):
            o = o + a[s * HG_SUB:(s + 1) * HG_SUB, :] * v[s:s + 1, :]
        bd = jnp.where(same_head, jnp.concatenate([st, st], axis=0), 0.0).astype(BF16)
        o = o + _dot_nt((q * jnp.exp(gc)).astype(BF16), bd)
        hi, lo = _split2(o * o)
        ms = (_dot(hi, ones_bd) + _dot(lo, ones_bd)) * (1.0 / HG_DV)
        y_ref[0, pl.ds(off, HG_SUB), lanes] = o * lax.rsqrt(ms + NORM_EPS) * gn * z
        gl = gc[HG_SUB - 1:HG_SUB, :]
        kt = (k * jnp.exp(gl - gc)).astype(BF16)
        full = _dot(v.astype(BF16).T, kt)
        fm = jnp.where(same_head, full, 0.0)
        st_scr[pr] = jnp.exp(gl) * st + fm[:HG_DV, :] + fm[HG_DV:, :]

    def sub(i, carry):
        off = pl.multiple_of(i * HG_SUB, HG_SUB)
        for pr in range(npair):
            sub_pair(off, pr)
        return carry

    lax.fori_loop(0, nsub, sub, 0, unroll=unroll)

    @pl.when(ci == pl.num_programs(1) - 1)
    def _():
        sout_ref[0] = st_scr[...]


def _hgrn(q, k, v, g, z, s0, gnorm, c):
    b, t, _ = q.shape
    hp = HG_HEADS // 2
    tok = pl.BlockSpec((1, c, HGW), lambda b_, i: (b_, i, 0))
    stt = pl.BlockSpec((1, hp, HG_DV, LANES), lambda b_, i: (b_, 0, 0, 0))
    gn = jnp.tile(gnorm.reshape(1, -1), (1, LANES // HG_DV))
    return pl.pallas_call(
        functools.partial(_hgrn_kernel, c=c, unroll=min(4, c // HG_SUB)),
        grid=(b, t // c),
        in_specs=[tok] * 5 + [stt, pl.BlockSpec((1, LANES), lambda b_, i: (0, 0))],
        out_specs=[tok, stt],
        out_shape=[jax.ShapeDtypeStruct((b, t, HGW), F32), jax.ShapeDtypeStruct((b, hp, HG_DV, LANES), F32)],
        scratch_shapes=[pltpu.VMEM((c, HGW), F32), pltpu.VMEM((hp, HG_DV, LANES), F32)],
        compiler_params=_params(("parallel", "arbitrary")), name="hgrn2",
    )(q, k, v, g, z, s0, gn)


def _state_to_t(s):
    b = s.shape[0]
    return s.reshape(b, HG_HEADS // 2, 2, HG_DK, HG_DV).transpose(0, 1, 4, 2, 3).reshape(b, HG_HEADS // 2, HG_DV, 2 * HG_DK)


def _state_from_t(st):
    b = st.shape[0]
    return st.reshape(b, HG_HEADS // 2, HG_DV, 2, HG_DK).transpose(0, 1, 3, 4, 2).reshape(b, HG_HEADS, HG_DK, HG_DV)


def _rel_bucket_np(rel):
    n = np.maximum(rel, 0)
    exact = REL_BUCKETS // 2
    scaled = np.log(np.maximum(n, 1).astype(np.float32) / exact) / math.log(REL_MAX_DIST / exact)
    large = np.minimum(exact + (np.maximum(scaled, 0.0) * (REL_BUCKETS - exact)).astype(np.int32), REL_BUCKETS - 1)
    return np.where(n < exact, n, large).astype(np.int32)


def _swa_kernel(relb_ref, sink_ref, bucket_ref, q_ref, kp_ref, kc_ref, vp_ref, vc_ref, o_ref, bias_scr,
                *, first_has_no_prev):
    bi = pl.program_id(1)
    w = WINDOW
    nq = q_ref.shape[1]

    @pl.when(jnp.logical_and(pl.program_id(0) == 0, bi == 0))
    def _():
        bucket = bucket_ref[...]
        for h in range(SW_HEADS):
            acc = jnp.zeros((nq, 2 * w), F32)
            for bk in range(REL_BUCKETS):
                acc = jnp.where(bucket == bk, relb_ref[bk, h], acc)
            bias_scr[h] = acc

    qi = lax.broadcasted_iota(jnp.int32, (nq, 2 * w), 0)
    kj = lax.broadcasted_iota(jnp.int32, (nq, 2 * w), 1)
    rel = qi + w - kj
    allowed = jnp.logical_and(rel >= 0, rel <= WINDOW)
    if first_has_no_prev:
        allowed = jnp.logical_and(allowed, jnp.logical_or(bi > 0, kj >= w))
    lane_hi = lax.broadcasted_iota(jnp.int32, (1, LANES), 1) >= SW_DIM
    k = jnp.concatenate([kp_ref[0], kc_ref[0]], axis=0)
    v = jnp.concatenate([vp_ref[0], vc_ref[0]], axis=0)
    ksw = pltpu.roll(k, SW_DIM, 1)
    vsw = pltpu.roll(v, SW_DIM, 1)
    kg = (jnp.where(lane_hi, ksw, k).astype(BF16), jnp.where(lane_hi, k, ksw).astype(BF16))
    vg = (jnp.where(lane_hi, vsw, v).astype(BF16), jnp.where(lane_hi, v, vsw).astype(BF16))
    q = q_ref[0]
    outs = []
    for pair in range(SW_HEADS // 2):
        qb = q[:, pair * LANES:(pair + 1) * LANES]
        zq = jnp.zeros_like(qb)
        res = []
        for half in range(2):
            h = 2 * pair + half
            g = h // SW_GROUP
            qm = jnp.where(lane_hi, qb, zq) if half else jnp.where(lane_hi, zq, qb)
            s = jnp.where(allowed, _dot_nt(qm, kg[g]) + bias_scr[h], MASK_VALUE)
            sink = sink_ref[h]
            m = jnp.maximum(jnp.max(s, axis=-1, keepdims=True), sink)
            e = jnp.exp(s - m)
            den = jnp.sum(e, axis=-1, keepdims=True) + jnp.exp(sink - m)
            res.append(_dot((e / den).astype(BF16), vg[g]))
        outs.append(jnp.where(lane_hi, res[1], res[0]))
    o_ref[0] = jnp.concatenate(outs, axis=1)


def _swa(q, kprev_src, kcur_src, vprev_src, vcur_src, rel_bias, sinks, prev_is_shifted):
    b, s, _ = q.shape
    w = WINDOW
    nb = kcur_src.shape[1] // w
    nq = s // nb
    kvw = SW_KV_HEADS * SW_DIM
    rel = (np.arange(nq)[:, None] + w) - np.arange(2 * w)[None, :]
    bucket = jnp.asarray(_rel_bucket_np(rel))
    cur = pl.BlockSpec((1, w, kvw), lambda b_, i: (b_, i, 0))
    prev = pl.BlockSpec((1, w, kvw), lambda b_, i: (b_, jnp.maximum(i - 1, 0), 0)) if prev_is_shifted else cur
    smem = pl.BlockSpec(memory_space=pltpu.SMEM)
    return pl.pallas_call(
        functools.partial(_swa_kernel, first_has_no_prev=prev_is_shifted),
        grid=(b, nb),
        in_specs=[smem, smem, pl.BlockSpec((nq, 2 * w), lambda b_, i: (0, 0)),
                  pl.BlockSpec((1, nq, SW_HEADS * SW_DIM), lambda b_, i: (b_, i, 0)), prev, cur, prev, cur],
        out_specs=pl.BlockSpec((1, nq, SW_HEADS * SW_DIM), lambda b_, i: (b_, i, 0)),
        out_shape=jax.ShapeDtypeStruct((b, s, SW_HEADS * SW_DIM), F32),
        scratch_shapes=[pltpu.VMEM((SW_HEADS, nq, 2 * w), F32)],
        compiler_params=_params(("arbitrary", "arbitrary")), name="swa",
    )(rel_bias, sinks, bucket, q, kprev_src, kcur_src, vprev_src, vcur_src)


def _tile(m, pref):
    return pref if m % pref == 0 else m


def kernel(x_prompt, x_sample, cache_mla_ckv, cache_mla_krope, cache_sb_k, cache_sb_v, state_hgrn, state_swa_k, state_swa_v, page_table, mix_norm, w_in_even, mla_q_norm, mla_w_qb, mla_kv_norm, mla_w_kvb, w_out_even, w_in_odd, hgrn_lb_logits, hgrn_out_norm, swa_sinks, w_out_odd, rel_bias, mlp_norm, w_up, w_down, final_norm):
    bp, sp, d = x_prompt.shape
    bs, ts, _ = x_sample.shape
    depth = mix_norm.shape[0]
    past = page_table.shape[1] * PAGE_SIZE
    mp, ms = bp * sp, bs * ts
    xp = x_prompt.reshape(mp, d)
    xs = x_sample.reshape(ms, d)
    tab_p = jnp.tile(_rope_tables(jnp.arange(sp)), (bp, 1))
    tab_s = jnp.tile(_rope_tables(past + jnp.arange(ts)), (bs, 1))
    tm_p, tm_s = _tile(mp, 256), _tile(ms, 256)
    t_att = _tile(sp, 256)
    npages = page_table.shape[1]
    pg = next(c for c in (64, 8, 1) if npages % c == 0)
    nbuf = state_swa_k.shape[2]
    hg_pad = HG_SUB - ts
    kvw = SW_KV_HEADS * SW_DIM
    own_head = (jnp.arange(SB_WIDTH)[None, :] // SB_DIM == jnp.arange(SB_HEADS)[:, None])

    outs_p = {k: [] for k in ("ckv", "kr", "sbk", "sbv", "hg", "swk", "swv")}
    outs_s = {k: [] for k in ("ckv", "kr", "sbk", "sbv", "hg", "swk", "swv")}
    for layer in range(depth):
        last = layer == depth - 1
        if layer % 2 == 0:
            e = layer // 2
            wts = _even_weights(w_in_even[e], mla_w_qb[e], mla_w_kvb[e])
            wvbd = wts[6]
            q_p, k_p, v_p, c_p, r_p, sq_p, sk_p, sv_p = _even_proj(
                xp, mix_norm[layer], wts, mla_q_norm[e], mla_kv_norm[e], tab_p, False, tm_p)
            q_s, ql_s, c_s, r_s, sq_s, sk_s, sv_s = _even_proj(
                xs, mix_norm[layer], wts, mla_q_norm[e], mla_kv_norm[e], tab_s, True, tm_s)
            a_p = _mla_prompt(q_p.reshape(bp, sp, -1), k_p.reshape(bp, sp, -1), v_p.reshape(bp, sp, -1),
                              _tile(sp, 512))
            b_p = _sb_prompt(sq_p.reshape(bp, sp, -1), sk_p.reshape(bp, sp, -1), sv_p.reshape(bp, sp, -1), t_att)
            nrow = ts * MLA_HEADS
            qlat = ql_s.reshape(bs, nrow, MLA_KV_LORA)
            qrope = q_s.reshape(bs, nrow, HEAD_PAD)[:, :, MLA_NOPE:MLA_NOPE + MLA_ROPE]
            olat = _mla_sample(page_table, qlat, qrope, c_s.reshape(bs, ts, -1), r_s.reshape(bs, ts, -1),
                               cache_mla_ckv, cache_mla_krope, e, pg)
            a_s = _mm(olat.reshape(ms, MLA_HEADS * MLA_KV_LORA), wvbd, F32)
            qm = jnp.where(own_head[None, None], sq_s.reshape(bs, ts, 1, SB_WIDTH), jnp.zeros((), BF16))
            o_sb = _sb_sample(page_table, qm.reshape(bs, nrow, SB_WIDTH), sk_s.reshape(bs, ts, -1),
                              sv_s.reshape(bs, ts, -1), cache_sb_k, cache_sb_v, e)
            b_s = o_sb.reshape(bs, ts, SB_HEADS, SB_WIDTH).sum(axis=2).reshape(ms, SB_WIDTH)
            a_p, b_p = a_p.reshape(mp, -1), b_p.reshape(mp, -1)
            w_out = w_out_even[e].astype(BF16)
            wa, wb = w_out[:MLA_HEADS * MLA_V], w_out[MLA_HEADS * MLA_V:]
            wa_p = jnp.pad(wa.reshape(MLA_HEADS, MLA_V, d), ((0, 0), (0, HEAD_PAD - MLA_V), (0, 0))).reshape(-1, d)
            outs_p["ckv"].append(c_p.reshape(bp, sp, -1)); outs_p["kr"].append(r_p.reshape(bp, sp, -1))
            outs_p["sbk"].append(sk_p.reshape(bp, sp, SB_HEADS, SB_DIM)); outs_p["sbv"].append(sv_p.reshape(bp, sp, SB_HEADS, SB_DIM))
            outs_s["ckv"].append(c_s.reshape(bs, ts, -1)); outs_s["kr"].append(r_s.reshape(bs, ts, -1))
            outs_s["sbk"].append(sk_s.reshape(bs, ts, SB_HEADS, SB_DIM)); outs_s["sbv"].append(sv_s.reshape(bs, ts, SB_HEADS, SB_DIM))
        else:
            o = layer // 2
            win = w_in_odd[o].astype(BF16)
            lbl = hgrn_lb_logits.astype(F32)
            hq_p, hk_p, hv_p, hg_p, hz_p, dq_p, dk_p, dv_p = _odd_proj(xp, mix_norm[layer], win, lbl, o, tm_p)
            hq_s, hk_s, hv_s, hg_s, hz_s, dq_s, dk_s, dv_s = _odd_proj(xs, mix_norm[layer], win, lbl, o, tm_s)
            r3 = lambda a, b_: a.reshape(b_, -1, a.shape[-1])
            s0_p = jnp.zeros((bp, HG_HEADS // 2, HG_DV, LANES), F32)
            a_p, st_p = _hgrn(r3(hq_p, bp), r3(hk_p, bp), r3(hv_p, bp), r3(hg_p, bp), r3(hz_p, bp), s0_p,
                              hgrn_out_norm[o], _tile(sp, 256))
            padt = lambda a: jnp.pad(r3(a, bs), ((0, 0), (0, hg_pad), (0, 0)))
            a_s, st_s = _hgrn(padt(hq_s), padt(hk_s), padt(hv_s), padt(hg_s), padt(hz_s),
                              _state_to_t(state_hgrn[o].astype(F32)), hgrn_out_norm[o], HG_SUB)
            a_s = a_s[:, :ts].reshape(ms, -1)
            dk3, dv3 = r3(dk_p, bp), r3(dv_p, bp)
            b_p = _swa(r3(dq_p, bp), dk3, dk3, dv3, dv3, rel_bias, swa_sinks[o], True)
            padw = lambda a: jnp.pad(r3(a, bs), ((0, 0), (0, WINDOW - ts), (0, 0)))
            kbuf = state_swa_k[o].reshape(bs, nbuf, kvw)
            vbuf = state_swa_v[o].reshape(bs, nbuf, kvw)
            b_s = _swa(padw(dq_s), kbuf, padw(dk_s), vbuf, padw(dv_s), rel_bias, swa_sinks[o], False)
            b_s = b_s[:, :ts].reshape(ms, -1)
            a_p, b_p = a_p.reshape(mp, -1), b_p.reshape(mp, -1)
            w_out = w_out_odd[o].astype(BF16)
            wa, wb = w_out[:HG_HEADS * HG_DV], w_out[HG_HEADS * HG_DV:]
            wa_p = wa
            nkeep = min(WINDOW, sp)
            outs_p["hg"].append(_state_from_t(st_p)); outs_s["hg"].append(_state_from_t(st_s))
            outs_p["swk"].append(dk3[:, -nkeep:].reshape(bp, nkeep, SW_KV_HEADS, SW_DIM))
            outs_p["swv"].append(dv3[:, -nkeep:].reshape(bp, nkeep, SW_KV_HEADS, SW_DIM))
            kk = jnp.concatenate([kbuf, r3(dk_s, bs)], axis=1)[:, -nbuf:]
            vv = jnp.concatenate([vbuf, r3(dv_s, bs)], axis=1)[:, -nbuf:]
            outs_s["swk"].append(kk.reshape(bs, nbuf, SW_KV_HEADS, SW_DIM))
            outs_s["swv"].append(vv.reshape(bs, nbuf, SW_KV_HEADS, SW_DIM))
        wu, wd = w_up[layer].astype(BF16), w_down[layer].astype(BF16)
        xp = _post(xp, a_p, b_p, wa_p, wb, mlp_norm[layer], wu, wd, final_norm, last, _tile(mp, 512), _tile(D_FF, 1024))
        xs = _post(xs, a_s, b_s, wa, wb, mlp_norm[layer], wu, wd, final_norm, last, _tile(ms, 512), _tile(D_FF, 1024))
    st = lambda lst: jnp.stack(lst, axis=0)
    return (xp.reshape(bp, sp, d), xs.reshape(bs, ts, d),
            st(outs_p["ckv"]), st(outs_p["kr"]), st(outs_p["sbk"]), st(outs_p["sbv"]), st(outs_p["hg"]),
            st(outs_p["swk"]), st(outs_p["swv"]),
            st(outs_s["ckv"]), st(outs_s["kr"]), st(outs_s["sbk"]), st(outs_s["sbv"]), st(outs_s["hg"]),
            st(outs_s["swk"]), st(outs_s["swv"]))
```
